```python
import jax, jax.numpy as jnp
from jax import lax
import numpy as np

D_MODEL = 1024
BATCH = 4
SEQ = 8192
DEPTH = 1
DEC_BATCH = 128
DEC_SEQ = 1
PAST_LEN = 8192
PAGE_SIZE = 128

D_MIX = D_MODEL
D_ATTN = D_MIX // 2
D_MLSTM = D_MIX - D_ATTN
ATTN_HEADS = 8
ATTN_HEAD_DIM = D_ATTN // ATTN_HEADS
MLSTM_HEADS = 4
MLSTM_HEAD_DIM = D_MLSTM // MLSTM_HEADS
MOBA_BLOCK = 256
MOBA_TOPK = 3
MOBA_QBLOCK = 32
MLSTM_CHUNK = 64
CONV_W = 4
PEER_HEADS = 8
PEER_N_KEYS = 128
PEER_N_EXPERTS = PEER_N_KEYS ** 2
PEER_TOPK = 16
PEER_KEY_DIM = 256
PEER_HALF = PEER_KEY_DIM // 2
PEER_BLOCK = 128
EPS = 1e-6
D_IN = 3 * D_ATTN + 4 * D_MLSTM + 2 * MLSTM_HEADS
IN_SPLITS = [D_ATTN, 2 * D_ATTN, 3 * D_ATTN, 3 * D_ATTN + 2 * D_MLSTM, 3 * D_ATTN + 3 * D_MLSTM, 3 * D_ATTN + 4 * D_MLSTM]

kernel_name = 'hymba_moba_mlstm_peer_adaln_step'


def _rmsnorm(x, g):
    xf = x.astype(jnp.float32)
    y = xf * lax.rsqrt(jnp.mean(xf * xf, axis=-1, keepdims=True) + EPS)
    return (y * g.astype(jnp.float32)).astype(x.dtype)


def _headnorm(x, g, n_heads):
    B, T, W = x.shape
    xf = x.reshape(B, T, n_heads, W // n_heads).astype(jnp.float32)
    y = xf * lax.rsqrt(jnp.mean(xf * xf, axis=-1, keepdims=True) + EPS)
    return (y.reshape(B, T, W) * g.astype(jnp.float32)).astype(x.dtype)


def _ada(c, w_ada, b_ada):
    mod = jax.nn.silu(c) @ w_ada + b_ada
    return jnp.split(mod[:, None, :], 6, axis=-1)


def _front(x, c, w_ada, b_ada, norm1_g, w_in, b_if):
    sh1, sc1, g1, sh2, sc2, g2 = _ada(c, w_ada, b_ada)
    h = _rmsnorm(x, norm1_g) * (1 + sc1) + sh1
    q_a, k_a, v_a, qk_m, v_m, o_m, if_m = jnp.split(h @ w_in, IN_SPLITS, axis=-1)
    return (g1, sh2, sc2, g2), q_a, k_a, v_a, qk_m, v_m, o_m, if_m + b_if


def _causal_conv(x, buf, w):
    T = x.shape[1]
    xp = jnp.concatenate([buf.astype(x.dtype), x], axis=1)
    y = sum(xp[:, j:j + T] * w[j] for j in range(CONV_W))
    return jax.nn.silu(y), xp[:, T:]


def _moba_attend(q, k_sel, v_sel, sel_mask, k_own, v_own, own_mask):
    scale = q.shape[-1] ** -0.5
    s_own = jnp.einsum('bhqd,bhmd->bhqm', q, k_own).astype(jnp.float32) * scale
    s_own = jnp.where(own_mask, s_own, -jnp.inf)
    if k_sel is None:
        p = jax.nn.softmax(s_own, axis=-1)
        return jnp.einsum('bhqm,bhmd->bhqd', p.astype(v_own.dtype), v_own)
    s_sel = jnp.einsum('bhqd,bhqnd->bhqn', q, k_sel).astype(jnp.float32) * scale
    s_sel = jnp.where(sel_mask, s_sel, -jnp.inf)
    p = jax.nn.softmax(jnp.concatenate([s_sel, s_own], axis=-1), axis=-1).astype(v_own.dtype)
    n = s_sel.shape[-1]
    return (jnp.einsum('bhqn,bhqnd->bhqd', p[..., :n], v_sel)
            + jnp.einsum('bhqm,bhmd->bhqd', p[..., n:], v_own))


def _moba_prompt(q, k, v):
    B, S, H, d = q.shape
    nb = -(-S // MOBA_BLOCK)
    pad = nb * MOBA_BLOCK - S
    def blocks(a):
        a = jnp.pad(a, ((0, 0), (0, pad), (0, 0), (0, 0)))
        return a.reshape(B, nb, MOBA_BLOCK, H, d).transpose(0, 3, 1, 2, 4)
    kb, vb = blocks(k), blocks(v)
    kmean = jnp.mean(kb, axis=3, dtype=jnp.float32)
    n_sel = min(MOBA_TOPK, nb)
    nq = S // MOBA_QBLOCK
    qs = q.reshape(B, nq, MOBA_QBLOCK, H, d).transpose(1, 0, 3, 2, 4)
    b_idx = jnp.arange(B)[:, None, None, None]
    h_idx = jnp.arange(H)[None, :, None, None]

    def step(args):
        qi, qb = args
        q0 = qi * MOBA_QBLOCK
        own = q0 // MOBA_BLOCK
        qpos = q0 + jnp.arange(MOBA_QBLOCK)
        kpos = own * MOBA_BLOCK + jnp.arange(MOBA_BLOCK)
        k_own = lax.dynamic_index_in_dim(kb, own, axis=2, keepdims=False)
        v_own = lax.dynamic_index_in_dim(vb, own, axis=2, keepdims=False)
        own_mask = kpos[None, :] <= qpos[:, None]
        s_blk = jnp.einsum('bhqd,bhnd->bhqn', qb.astype(jnp.float32), kmean)
        s_blk = jnp.where(jnp.arange(nb) < own, s_blk, -jnp.inf)
        top_s, top_i = lax.top_k(s_blk, n_sel)
        k_sel = kb[b_idx, h_idx, top_i].reshape(B, H, MOBA_QBLOCK, n_sel * MOBA_BLOCK, d)
        v_sel = vb[b_idx, h_idx, top_i].reshape(B, H, MOBA_QBLOCK, n_sel * MOBA_BLOCK, d)
        sel_mask = jnp.repeat(jnp.isfinite(top_s), MOBA_BLOCK, axis=-1)
        return _moba_attend(qb, k_sel, v_sel, sel_mask, k_own, v_own, own_mask)

    out = lax.map(step, (jnp.arange(nq), qs))
    return out.transpose(1, 0, 3, 2, 4).reshape(B, S, H * d)


def _moba_sample(q, k_new, v_new, k_pool, v_pool, page_table, layer):
    DB, T, H, d = q.shape
    PS = k_pool.shape[2]
    n_pages = page_table.shape[1]
    ppb = MOBA_BLOCK // PS
    nbp = (n_pages * PS) // MOBA_BLOCK
    tail_len = (n_pages - nbp * ppb) * PS
    assert tail_len + T <= MOBA_BLOCK
    qh = q.transpose(0, 2, 1, 3)
    tail_phys = page_table[:, nbp * ppb:]
    k_tail = k_pool[layer, tail_phys].reshape(DB, tail_len, H, d).astype(k_new.dtype)
    v_tail = v_pool[layer, tail_phys].reshape(DB, tail_len, H, d).astype(v_new.dtype)
    k_own = jnp.concatenate([k_tail, k_new], axis=1).transpose(0, 2, 1, 3)
    v_own = jnp.concatenate([v_tail, v_new], axis=1).transpose(0, 2, 1, 3)
    own_mask = jnp.arange(tail_len + T)[None, :] <= (tail_len + jnp.arange(T))[:, None]
    if nbp == 0:
        out = _moba_attend(qh, None, None, None, k_own, v_own, own_mask)
        return out.transpose(0, 2, 1, 3).reshape(DB, T, H * d)
    page_sum = jnp.sum(k_pool, axis=2, dtype=jnp.float32)[layer]
    kmean = page_sum[page_table[:, :nbp * ppb]].reshape(DB, nbp, ppb, H, d).sum(axis=2) / MOBA_BLOCK
    kmean = kmean.transpose(0, 2, 1, 3)
    s_blk = jnp.einsum('bhtd,bhnd->bhtn', qh.astype(jnp.float32), kmean)
    n_sel = min(MOBA_TOPK, nbp)
    _, top_i = lax.top_k(s_blk, n_sel)
    logical = top_i[..., None] * ppb + jnp.arange(ppb)
    phys = page_table[jnp.arange(DB)[:, None, None, None, None], logical]
    h_idx = jnp.arange(H)[None, :, None, None, None]
    s_idx = jnp.arange(PS)
    k_sel = k_pool[layer, phys[..., None], s_idx, h_idx[..., None]].reshape(DB, H, T, n_sel * MOBA_BLOCK, d)
    v_sel = v_pool[layer, phys[..., None], s_idx, h_idx[..., None]].reshape(DB, H, T, n_sel * MOBA_BLOCK, d)
    sel_mask = jnp.ones((DB, H, T, n_sel * MOBA_BLOCK), bool)
    out = _moba_attend(qh, k_sel.astype(k_new.dtype), v_sel.astype(v_new.dtype), sel_mask, k_own, v_own, own_mask)
    return out.transpose(0, 2, 1, 3).reshape(DB, T, H * d)


def _mlstm_heads(qk_c, v_m, if_m):
    B, T, _ = v_m.shape
    def heads(a):
        return a.reshape(B, T, MLSTM_HEADS, MLSTM_HEAD_DIM).transpose(0, 2, 1, 3).astype(jnp.float32)
    q_m, k_m = jnp.split(qk_c, 2, axis=-1)
    gates = if_m.astype(jnp.float32).transpose(0, 2, 1)
    return (heads(q_m), heads(k_m) * MLSTM_HEAD_DIM ** -0.5, heads(v_m),
            gates[:, :MLSTM_HEADS], gates[:, MLSTM_HEADS:])


def _mlstm_chunk(carry, inp):
    C, n, m = carry
    q, k, v, ig, fg = inp
    L = q.shape[2]
    b = jnp.cumsum(jax.nn.log_sigmoid(fg), axis=-1)
    causal = jnp.tril(jnp.ones((L, L), bool))
    D = jnp.where(causal, b[..., :, None] - b[..., None, :] + ig[..., None, :], -jnp.inf)
    inter = b + m[..., None]
    m_t = jnp.maximum(inter, jnp.max(D, axis=-1))
    W = jnp.exp(D - m_t[..., None])
    w0 = jnp.exp(inter - m_t)
    Sqk = jnp.einsum('bhtd,bhsd->bhts', q, k) * W
    num = w0[..., None] * jnp.einsum('bhvk,bhtk->bhtv', C, q) + jnp.einsum('bhts,bhsv->bhtv', Sqk, v)
    den = w0 * jnp.einsum('bhk,bhtk->bht', n, q) + jnp.sum(Sqk, axis=-1)
    h = num / jnp.maximum(jnp.abs(den), jnp.exp(-m_t))[..., None]
    m_new = m_t[..., -1]
    g = jnp.exp(b[..., -1:] - b + ig - m_new[..., None])
    decay = jnp.exp(b[..., -1] + m - m_new)
    C_new = decay[..., None, None] * C + jnp.einsum('bhs,bhsv,bhsk->bhvk', g, v, k)
    n_new = decay[..., None] * n + jnp.einsum('bhs,bhsk->bhk', g, k)
    return (C_new, n_new, m_new), h


def _mlstm_prompt(q, k, v, ig, fg):
    B, H, S, d = q.shape
    nc = S // MLSTM_CHUNK
    def to_chunks(a):
        return jnp.moveaxis(a.reshape(a.shape[:2] + (nc, MLSTM_CHUNK) + a.shape[3:]), 2, 0)
    xs = (to_chunks(q), to_chunks(k), to_chunks(v), to_chunks(ig), to_chunks(fg))
    init = (jnp.zeros((B, H, d, d), jnp.float32), jnp.zeros((B, H, d), jnp.float32), jnp.zeros((B, H), jnp.float32))
    state, h = lax.scan(_mlstm_chunk, init, xs)
    return state, jnp.moveaxis(h, 0, 2).reshape(B, H, S, d)


def _merge_heads(h, dtype):
    B, H, T, d = h.shape
    return h.transpose(0, 2, 1, 3).reshape(B, T, H * d).astype(dtype)


def _peer(h, w_query, sub_keys, expert_u, expert_v):
    lead = h.shape[:-1]
    x = h.reshape(-1, D_MODEL)
    T = x.shape[0]
    nb = -(-T // PEER_BLOCK)
    xb = jnp.pad(x, ((0, nb * PEER_BLOCK - T), (0, 0))).reshape(nb, PEER_BLOCK, D_MODEL)

    def block(xt):
        q = (xt @ w_query).reshape(PEER_BLOCK, PEER_HEADS, 2, PEER_HALF)
        s = jnp.einsum('thpd,hpnd->thpn', q, sub_keys).astype(jnp.float32)
        s_top, i_top = lax.top_k(s, PEER_TOPK)
        cand = (s_top[:, :, 0, :, None] + s_top[:, :, 1, None, :]).reshape(PEER_BLOCK, PEER_HEADS, PEER_TOPK * PEER_TOPK)
        c_s, c_i = lax.top_k(cand, PEER_TOPK)
        e1 = jnp.take_along_axis(i_top[:, :, 0], c_i // PEER_TOPK, axis=-1)
        e2 = jnp.take_along_axis(i_top[:, :, 1], c_i % PEER_TOPK, axis=-1)
        idx = e1 * PEER_N_KEYS + e2
        g = jax.nn.softmax(c_s, axis=-1)
        act = jax.nn.gelu(jnp.einsum('td,thkd->thk', xt, expert_u[idx]).astype(jnp.float32), approximate=False)
        return jnp.einsum('thk,thkd->td', (g * act).astype(xt.dtype), expert_v[idx])

    y = lax.map(block, xb).reshape(nb * PEER_BLOCK, D_MODEL)[:T]
    return y.reshape(lead + (D_MODEL,))


def _back(x, mods, attn, h_m, o_m, attn_norm_g, mlstm_norm_g, w_out, norm2_g, w_query, sub_keys, expert_u, expert_v):
    g1, sh2, sc2, g2 = mods
    a = _headnorm(attn, attn_norm_g, ATTN_HEADS)
    mm = jax.nn.sigmoid(o_m) * _headnorm(h_m, mlstm_norm_g, MLSTM_HEADS)
    x = x + g1 * (jnp.concatenate([a, mm], axis=-1) @ w_out)
    h = _rmsnorm(x, norm2_g) * (1 + sc2) + sh2
    return x + g2 * _peer(h, w_query, sub_keys, expert_u, expert_v)


def setup_inputs(seed: int = 0) -> dict:
    key = jax.random.key(seed)
    ks = jax.random.split(key, 28)
    f32 = jnp.float32
    n_pages = PAST_LEN // PAGE_SIZE
    n_used = DEC_BATCH * n_pages
    n_pool = n_used + n_used // 4
    L = DEPTH
    def nrm(k, shape, scale=1.0):
        return jax.random.normal(k, shape, f32) * scale
    def gain(k, shape):
        return 1.0 + nrm(k, shape, 0.02)
    b_i = nrm(ks[15], (L, MLSTM_HEADS), 0.1)
    b_f = jnp.linspace(3.0, 6.0, MLSTM_HEADS, dtype=f32)[None, :] + nrm(ks[16], (L, MLSTM_HEADS), 0.1)
    return {
        'x_prompt': nrm(ks[0], (BATCH, SEQ, D_MODEL)),
        'x_sample': nrm(ks[1], (DEC_BATCH, DEC_SEQ, D_MODEL)),
        'cache_k_pool': nrm(ks[2], (L, n_pool, PAGE_SIZE, ATTN_HEADS, ATTN_HEAD_DIM)),
        'cache_v_pool': nrm(ks[3], (L, n_pool, PAGE_SIZE, ATTN_HEADS, ATTN_HEAD_DIM)),
        'page_table': jax.random.permutation(ks[4], n_pool)[:n_used].reshape(DEC_BATCH, n_pages).astype(jnp.int32),
        'state_mlstm_C': nrm(ks[5], (L, DEC_BATCH, MLSTM_HEADS, MLSTM_HEAD_DIM, MLSTM_HEAD_DIM), 0.1),
        'state_mlstm_n': nrm(ks[6], (L, DEC_BATCH, MLSTM_HEADS, MLSTM_HEAD_DIM), 0.1),
        'state_mlstm_m': nrm(ks[7], (L, DEC_BATCH, MLSTM_HEADS), 0.5),
        'state_conv': nrm(ks[8], (L, DEC_BATCH, CONV_W - 1, 2 * D_MLSTM)),
        'c_prompt': nrm(ks[9], (BATCH, D_MODEL)),
        'c_sample': nrm(ks[10], (DEC_BATCH, D_MODEL)),
        'w_ada': nrm(ks[11], (L, D_MODEL, 6 * D_MODEL), 0.5 * D_MODEL ** -0.5),
        'b_ada': nrm(ks[12], (L, 6 * D_MODEL), 0.02),
        'norm1_g': gain(ks[13], (L, D_MODEL)),
        'w_in': nrm(ks[14], (L, D_MODEL, D_IN), D_MODEL ** -0.5),
        'b_if': jnp.concatenate([b_i, b_f], axis=-1),
        'conv_w': nrm(ks[17], (L, CONV_W, 2 * D_MLSTM), CONV_W ** -0.5),
        'attn_norm_g': gain(ks[18], (L, D_ATTN)),
        'mlstm_norm_g': gain(ks[19], (L, D_MLSTM)),
        'w_out': nrm(ks[20], (L, D_MIX, D_MODEL), D_MIX ** -0.5),
        'norm2_g': gain(ks[21], (L, D_MODEL)),
        'peer_w_query': nrm(ks[22], (L, D_MODEL, PEER_HEADS * PEER_KEY_DIM), D_MODEL ** -0.5),
        'peer_sub_keys': nrm(ks[23], (L, PEER_HEADS, 2, PEER_N_KEYS, PEER_HALF), PEER_HALF ** -0.5),
        'peer_u': nrm(ks[24], (L, PEER_N_EXPERTS, D_MODEL), D_MODEL ** -0.5),
        'peer_v': nrm(ks[25], (L, PEER_N_EXPERTS, D_MODEL), 0.5),
        'normf_g': gain(ks[26], (D_MODEL,)),
    }


def reference(x_prompt, x_sample, cache_k_pool, cache_v_pool, page_table,
              state_mlstm_C, state_mlstm_n, state_mlstm_m, state_conv,
              c_prompt, c_sample, w_ada, b_ada, norm1_g, w_in, b_if, conv_w,
              attn_norm_g, mlstm_norm_g, w_out, norm2_g,
              peer_w_query, peer_sub_keys, peer_u, peer_v, normf_g):
    f32 = jnp.float32
    B, S, _ = x_prompt.shape
    DB, T, _ = x_sample.shape
    yp, ys = x_prompt, x_sample
    kp_l, vp_l, ks_l, vs_l = [], [], [], []
    Cp_l, np_l, mp_l, cvp_l = [], [], [], []
    Cs_l, ns_l, ms_l, cvs_l = [], [], [], []
    for l in range(DEPTH):
        peer_w = (peer_w_query[l], peer_sub_keys[l], peer_u[l], peer_v[l])
        mods, q_a, k_a, v_a, qk_m, v_m, o_m, if_m = _front(yp, c_prompt, w_ada[l], b_ada[l], norm1_g[l], w_in[l], b_if[l])
        ka = k_a.reshape(B, S, ATTN_HEADS, ATTN_HEAD_DIM)
        va = v_a.reshape(B, S, ATTN_HEADS, ATTN_HEAD_DIM)
        attn = _moba_prompt(q_a.reshape(B, S, ATTN_HEADS, ATTN_HEAD_DIM), ka, va)
        qk_c, conv_p = _causal_conv(qk_m, jnp.zeros((B, CONV_W - 1, 2 * D_MLSTM), yp.dtype), conv_w[l])
        q, k, v, ig, fg = _mlstm_heads(qk_c, v_m, if_m)
        (C_p, n_p, m_p), h_m = _mlstm_prompt(q, k, v, ig, fg)
        yp = _back(yp, mods, attn, _merge_heads(h_m, yp.dtype), o_m, attn_norm_g[l], mlstm_norm_g[l], w_out[l], norm2_g[l], *peer_w)
        kp_l.append(ka); vp_l.append(va)
        Cp_l.append(C_p); np_l.append(n_p); mp_l.append(m_p); cvp_l.append(conv_p)
        mods, q_a, k_a, v_a, qk_m, v_m, o_m, if_m = _front(ys, c_sample, w_ada[l], b_ada[l], norm1_g[l], w_in[l], b_if[l])
        kn = k_a.reshape(DB, T, ATTN_HEADS, ATTN_HEAD_DIM)
        vn = v_a.reshape(DB, T, ATTN_HEADS, ATTN_HEAD_DIM)
        attn = _moba_sample(q_a.reshape(DB, T, ATTN_HEADS, ATTN_HEAD_DIM), kn, vn, cache_k_pool, cache_v_pool, page_table, l)
        qk_c, conv_s = _causal_conv(qk_m, state_conv[l], conv_w[l])
        st0 = (state_mlstm_C[l].astype(f32), state_mlstm_n[l].astype(f32), state_mlstm_m[l].astype(f32))
        (C_s, n_s, m_s), h_s = _mlstm_chunk(st0, _mlstm_heads(qk_c, v_m, if_m))
        ys = _back(ys, mods, attn, _merge_heads(h_s, ys.dtype), o_m, attn_norm_g[l], mlstm_norm_g[l], w_out[l], norm2_g[l], *peer_w)
        ks_l.append(kn); vs_l.append(vn)
        Cs_l.append(C_s); ns_l.append(n_s); ms_l.append(m_s); cvs_l.append(conv_s)
    y_prompt = _rmsnorm(yp, normf_g)
    y_sample = _rmsnorm(ys, normf_g)
    k_prompt, v_prompt = jnp.stack(kp_l), jnp.stack(vp_l)
    k_sample, v_sample = jnp.stack(ks_l), jnp.stack(vs_l)
    C_prompt, n_prompt, m_prompt, conv_prompt = jnp.stack(Cp_l), jnp.stack(np_l), jnp.stack(mp_l), jnp.stack(cvp_l)
    C_sample, n_sample, m_sample, conv_sample = jnp.stack(Cs_l), jnp.stack(ns_l), jnp.stack(ms_l), jnp.stack(cvs_l)
    return (y_prompt, y_sample, k_prompt, v_prompt, k_sample, v_sample, C_prompt, n_prompt, m_prompt, conv_prompt, C_sample, n_sample, m_sample, conv_sample)
```

```python
import functools

import jax
import jax.numpy as jnp
from jax import lax
from jax.experimental import pallas as pl
from jax.experimental.pallas import tpu as pltpu

F32 = jnp.float32
BF16 = jnp.bfloat16
I32 = jnp.int32
HI = lax.Precision.HIGHEST

SUBLANES = 8
LANES = 128
VMEM_LIMIT_BYTES = 56 * 1024 * 1024

D_MODEL = 1024
D_ATTN = 512
D_MLSTM = 512
ATTN_HEADS = 8
ATTN_HEAD_DIM = 64
MLSTM_HEADS = 4
MLSTM_HEAD_DIM = 128
MOBA_BLOCK = 256
MOBA_TOPK = 3
CONV_W = 4
PEER_HEADS = 8
PEER_N_KEYS = 128
PEER_TOPK = 16
PEER_HALF = 128
EPS = 1e-6
N_GATES = 2 * MLSTM_HEADS
C_Q, C_K, C_V, C_QK, C_VM, C_OM, C_IF = 0, 512, 1024, 1536, 2560, 3072, 3584
NEG = -1e30


def _cparams(sem):
    return pltpu.CompilerParams(dimension_semantics=sem, vmem_limit_bytes=VMEM_LIMIT_BYTES)


def _nt(a, b, **kw):
    return lax.dot_general(a, b, (((1,), (1,)), ((), ())), preferred_element_type=F32, **kw)


def _tn(a, b, **kw):
    return lax.dot_general(a, b, (((0,), (0,)), ((), ())), preferred_element_type=F32, **kw)


def _silu(x):
    return x * jax.nn.sigmoid(x)


def _log_sigmoid(x):
    return jnp.minimum(x, 0.0) - jnp.log1p(jnp.exp(-jnp.abs(x)))


def _iota(shape, dim):
    return lax.broadcasted_iota(I32, shape, dim)


def _ada_body(c_ref, w_ref, b_ref, o_ref):
    o_ref[...] = jnp.dot(_silu(c_ref[...]), w_ref[...], precision=HI, preferred_element_type=F32) + b_ref[...]


def _ada(c, w_ada, b_ada):
    rows, d = c.shape
    n = w_ada.shape[1]
    tn = 512
    return pl.pallas_call(
        _ada_body,
        grid=(n // tn,),
        in_specs=[pl.BlockSpec((rows, d), lambda j: (0, 0)),
                  pl.BlockSpec((d, tn), lambda j: (0, j)),
                  pl.BlockSpec((1, tn), lambda j: (0, j))],
        out_specs=pl.BlockSpec((rows, tn), lambda j: (0, j)),
        out_shape=jax.ShapeDtypeStruct((rows, n), F32),
        compiler_params=_cparams(("arbitrary",)),
        name="ada_mod",
    )(c, w_ada, b_ada.reshape(1, n))


def _front_body(x_ref, sc_ref, sh_ref, g_ref, w_ref, wif_ref, wift_ref, bif_ref, bift_ref,
                q_ref, k_ref, v_ref, qb_ref, kb_ref, vb_ref, qk_ref, vm_ref, om_ref, if_ref, ift_ref):
    x = x_ref[0]
    h = x * lax.rsqrt(jnp.mean(x * x, axis=-1, keepdims=True) + EPS) * g_ref[...]
    h = h * (1.0 + sc_ref[0]) + sh_ref[0]
    hb = h.astype(BF16)

    def seg(lo, hi):
        return jnp.dot(hb, w_ref[:, lo:hi], preferred_element_type=F32)

    q = seg(C_Q, C_K)
    q_ref[0] = q
    qb_ref[0] = (q * (ATTN_HEAD_DIM ** -0.5)).astype(BF16)
    k = seg(C_K, C_V)
    k_ref[0] = k
    kb_ref[0] = k.astype(BF16)
    v = seg(C_V, C_QK)
    v_ref[0] = v
    vb_ref[0] = v.astype(BF16)
    qk_ref[0] = seg(C_QK, C_VM)
    vm_ref[0] = seg(C_VM, C_OM)
    om_ref[0] = seg(C_OM, C_IF)
    if_ref[0] = jnp.dot(h, wif_ref[...], precision=HI, preferred_element_type=F32) + bif_ref[...]
    ift_ref[0] = _nt(wift_ref[...], h, precision=HI) + bift_ref[...]


def _front(x, sc, sh, norm_g, w_in, b_if, tm):
    b, s, d = x.shape
    per_row = sc.shape[1] != 1
    mod_spec = (pl.BlockSpec((1, tm, d), lambda bi, i: (bi, i, 0)) if per_row
                else pl.BlockSpec((1, 1, d), lambda bi, i: (bi, 0, 0)))
    w_main = w_in[:, :C_IF].astype(BF16)
    w_if = jnp.pad(w_in[:, C_IF:], ((0, 0), (0, LANES - N_GATES)))
    w_ift = w_in[:, C_IF:].T
    b_row = jnp.pad(b_if, (0, LANES - N_GATES)).reshape(1, LANES)
    b_col = b_if.reshape(N_GATES, 1)
    const = lambda shape: pl.BlockSpec(shape, lambda bi, i: (0,) * len(shape))
    tok = lambda w: pl.BlockSpec((1, tm, w), lambda bi, i: (bi, i, 0))
    out_shapes = [
        jax.ShapeDtypeStruct((b, s, D_ATTN), F32),
        jax.ShapeDtypeStruct((b, s, D_ATTN), F32),
        jax.ShapeDtypeStruct((b, s, D_ATTN), F32),
        jax.ShapeDtypeStruct((b, s, D_ATTN), BF16),
        jax.ShapeDtypeStruct((b, s, D_ATTN), BF16),
        jax.ShapeDtypeStruct((b, s, D_ATTN), BF16),
        jax.ShapeDtypeStruct((b, s, 2 * D_MLSTM), F32),
        jax.ShapeDtypeStruct((b, s, D_MLSTM), F32),
        jax.ShapeDtypeStruct((b, s, D_MLSTM), F32),
        jax.ShapeDtypeStruct((b, s, LANES), F32),
        jax.ShapeDtypeStruct((b, N_GATES, s), F32),
    ]
    out_specs = [tok(D_ATTN)] * 6 + [tok(2 * D_MLSTM), tok(D_MLSTM), tok(D_MLSTM), tok(LANES),
                                     pl.BlockSpec((1, N_GATES, tm), lambda bi, i: (bi, 0, i))]
    return pl.pallas_call(
        _front_body,
        grid=(b, s // tm),
        in_specs=[tok(d), mod_spec, mod_spec, const((1, d)), const((d, C_IF)), const((d, LANES)),
                  const((N_GATES, d)), const((1, LANES)), const((N_GATES, 1))],
        out_specs=out_specs,
        out_shape=out_shapes,
        compiler_params=_cparams(("arbitrary", "arbitrary")),
        name="front",
    )(x, sc, sh, norm_g.reshape(1, d), w_main, w_if, w_ift, b_row, b_col)


def _km_body(k_ref, km_ref, *, nb):
    n = pl.program_id(1)

    @pl.when(n == 0)
    def _():
        km_ref[...] = jnp.zeros_like(km_ref)

    lw = km_ref.shape[2]
    ones = jnp.full((MOBA_BLOCK, lw), 1.0 / MOBA_BLOCK, F32)
    mean_b = _tn(k_ref[0], ones, precision=HI)
    row = _iota((D_ATTN, lw), 0)
    col = _iota((D_ATTN, lw), 1)
    km_ref[0] += jnp.where(col == (row // ATTN_HEAD_DIM) * nb + n, mean_b, 0.0)


def _block_means(k, nb, lw):
    b = k.shape[0]
    return pl.pallas_call(
        functools.partial(_km_body, nb=nb),
        grid=(b, nb),
        in_specs=[pl.BlockSpec((1, MOBA_BLOCK, D_ATTN), lambda bi, n: (bi, n, 0))],
        out_specs=pl.BlockSpec((1, D_ATTN, lw), lambda bi, n: (bi, 0, 0)),
        out_shape=jax.ShapeDtypeStruct((b, D_ATTN, lw), F32),
        compiler_params=_cparams(("arbitrary", "arbitrary")),
        name="moba_block_means",
    )(k)


def _sel_body(q_ref, km_ref, o_ref, *, nb):
    own = pl.program_id(1)
    s = jnp.dot(q_ref[0], km_ref[0], precision=HI, preferred_element_type=F32)
    shape = s.shape
    lw = shape[1]
    lane = _iota(shape, 1)
    blk = lane & (nb - 1)
    head = lane >> (nb.bit_length() - 1)
    valid = (blk < own) & (head < ATTN_HEADS)
    sel = jnp.zeros(shape, jnp.bool_)
    for h in range(ATTN_HEADS):
        sm = jnp.where(valid & (head == h), s, -jnp.inf)
        for _ in range(min(MOBA_TOPK, nb)):
            m = jnp.max(sm, axis=-1, keepdims=True)
            i = jnp.min(jnp.where(sm == m, lane, lw), axis=-1, keepdims=True)
            hit = lane == i
            sel = sel | (hit & (m > -jnp.inf))
            sm = jnp.where(hit, -jnp.inf, sm)
    o_ref[0] = jnp.where(sel, 0.0, NEG)


def _moba_select(q, km, nb):
    b, s, _ = q.shape
    lw = km.shape[2]
    return pl.pallas_call(
        functools.partial(_sel_body, nb=nb),
        grid=(b, s // MOBA_BLOCK),
        in_specs=[pl.BlockSpec((1, MOBA_BLOCK, D_ATTN), lambda bi, i: (bi, i, 0)),
                  pl.BlockSpec((1, D_ATTN, lw), lambda bi, i: (bi, 0, 0))],
        out_specs=pl.BlockSpec((1, MOBA_BLOCK, lw), lambda bi, i: (bi, i, 0)),
        out_shape=jax.ShapeDtypeStruct((b, s, lw), F32),
        compiler_params=_cparams(("arbitrary", "arbitrary")),
        name="moba_select",
    )(q, km)


def _attn_body(q_ref, k_ref, v_ref, sb_ref, o_ref, *, nb):
    pair = pl.program_id(1)
    own = pl.program_id(2)
    tq = MOBA_BLOCK
    q = q_ref[0]
    lane = _iota((tq, LANES), 1)
    low = lane < ATTN_HEAD_DIM
    zero = jnp.zeros_like(q)
    qs = (jnp.where(low, q, zero), jnp.where(low, zero, q))
    sb = sb_ref[0]
    sb_lane = _iota(sb.shape, 1)
    row = _iota((tq, tq), 0)
    col = _iota((tq, tq), 1)

    def attend(j, state, bias_fn):
        start = pl.multiple_of(j * tq, tq)
        kj = k_ref[0, pl.ds(start, tq), :]
        vj = v_ref[0, pl.ds(start, tq), :]
        new = []
        for hh in range(2):
            m, l, acc = state[hh]
            s = bias_fn(hh, _nt(qs[hh], kj))
            m_new = jnp.maximum(m, jnp.max(s, axis=-1, keepdims=True))
            alpha = jnp.exp(m - m_new)
            p = jnp.exp(s - m_new)
            l = alpha * l + jnp.sum(p, axis=-1, keepdims=True)
            acc = alpha * acc + jnp.dot(p.astype(BF16), vj, preferred_element_type=F32)
            new.append((m_new, l, acc))
        return tuple(new)

    def causal(hh, s):
        return jnp.where(col <= row, s, -jnp.inf)

    init = tuple((jnp.full((tq, 1), -jnp.inf, F32), jnp.zeros((tq, 1), F32), jnp.zeros((tq, LANES), F32))
                 for _ in range(2))
    state = attend(own, init, causal)

    def past(j, state):
        def biased(hh, s):
            c = (2 * pair + hh) * nb + j
            bias = jnp.max(jnp.where(sb_lane == c, sb, -jnp.inf), axis=-1, keepdims=True)
            return s + bias
        return attend(j, state, biased)

    state = lax.fori_loop(0, own, past, state)
    (_, l0, a0), (_, l1, a1) = state
    o_ref[0] = jnp.where(low, a0 / l0, a1 / l1)


def _moba_attend(qb, kb, vb, selbias, nb):
    b, s, _ = qb.shape
    lw = selbias.shape[2]
    pairs = D_ATTN // LANES
    return pl.pallas_call(
        functools.partial(_attn_body, nb=nb),
        grid=(b, pairs, s // MOBA_BLOCK),
        in_specs=[pl.BlockSpec((1, MOBA_BLOCK, LANES), lambda bi, p, i: (bi, i, p)),
                  pl.BlockSpec((1, s, LANES), lambda bi, p, i: (bi, 0, p)),
                  pl.BlockSpec((1, s, LANES), lambda bi, p, i: (bi, 0, p)),
                  pl.BlockSpec((1, MOBA_BLOCK, lw), lambda bi, p, i: (bi, i, 0))],
        out_specs=pl.BlockSpec((1, MOBA_BLOCK, LANES), lambda bi, p, i: (bi, i, p)),
        out_shape=jax.ShapeDtypeStruct((b, s, D_ATTN), F32),
        compiler_params=_cparams(("arbitrary", "arbitrary", "arbitrary")),
        name="moba_attend",
    )(qb, kb, vb, selbias)


def _moba_prompt(q, k, qb, kb, vb):
    s = q.shape[1]
    assert s % MOBA_BLOCK == 0
    nb = s // MOBA_BLOCK
    assert nb & (nb - 1) == 0, "block count must be a power of two"
    lw = -(-ATTN_HEADS * nb // LANES) * LANES
    km = _block_means(k, nb, lw)
    selbias = _moba_select(q, km, nb)
    return _moba_attend(qb, kb, vb, selbias, nb)


def _mlstm_body(qk_ref, vm_ref, if_ref, ift_ref, cw_ref, h_ref, c_ref, n_ref, m_ref, prev_scr):
    i = pl.program_id(1)
    L = qk_ref.shape[1]

    @pl.when(i == 0)
    def _():
        prev_scr[...] = jnp.zeros_like(prev_scr)
        c_ref[...] = jnp.zeros_like(c_ref)
        n_ref[...] = jnp.zeros_like(n_ref)
        m_ref[...] = jnp.zeros_like(m_ref)

    x = qk_ref[0]
    prev = prev_scr[...]
    w = cw_ref[...]
    rows = _iota(x.shape, 0)
    y = x * w[CONV_W - 1:CONV_W]
    for sft in range(1, CONV_W):
        xs = jnp.where(rows >= sft, pltpu.roll(x, sft, 0), pltpu.roll(prev, sft, 0))
        y = y + xs * w[CONV_W - 1 - sft:CONV_W - sft]
    prev_scr[...] = x
    qkc = _silu(y)

    gcol = if_ref[0]
    grow = ift_ref[0]
    tri_r = _iota((L, L), 0)
    tri_c = _iota((L, L), 1)
    lower = tri_c <= tri_r
    b_col = jnp.dot(lower.astype(F32), _log_sigmoid(gcol), precision=HI, preferred_element_type=F32)
    b_row = jnp.dot(_log_sigmoid(grow), (tri_r <= tri_c).astype(F32), precision=HI, preferred_element_type=F32)
    m_all = m_ref[0]
    lane = _iota(m_all.shape, 1)
    for h in range(MLSTM_HEADS):
        sl = slice(h * MLSTM_HEAD_DIM, (h + 1) * MLSTM_HEAD_DIM)
        qh = qkc[:, sl]
        kh = qkc[:, D_MLSTM + h * MLSTM_HEAD_DIM:D_MLSTM + (h + 1) * MLSTM_HEAD_DIM] * (MLSTM_HEAD_DIM ** -0.5)
        vh = vm_ref[0][:, sl]
        fh = MLSTM_HEADS + h
        bc = b_col[:, fh:fh + 1]
        br = b_row[fh:fh + 1, :]
        ic = gcol[:, h:h + 1]
        ir = grow[h:h + 1, :]
        m_prev = m_all[:, h:h + 1]
        c_prev = c_ref[0, h]
        n_prev = n_ref[0, h:h + 1, :]
        dmat = jnp.where(lower, bc - br + ir, -jnp.inf)
        inter = bc + m_prev
        m_t = jnp.maximum(inter, jnp.max(dmat, axis=-1, keepdims=True))
        wmat = jnp.exp(dmat - m_t)
        w0 = jnp.exp(inter - m_t)
        qb, kb, vb = qh.astype(BF16), kh.astype(BF16), vh.astype(BF16)
        sqk = _nt(qb, kb) * wmat
        num = w0 * _nt(qb, c_prev.astype(BF16)) + jnp.dot(sqk.astype(BF16), vb, preferred_element_type=F32)
        den = w0 * jnp.sum(qh * n_prev, axis=-1, keepdims=True) + jnp.sum(sqk, axis=-1, keepdims=True)
        h_ref[0, :, sl] = num / jnp.maximum(jnp.abs(den), jnp.exp(-m_t))
        m_new = m_t[L - 1:L, :]
        b_last = bc[L - 1:L, :]
        g = jnp.exp(b_last - bc + ic - m_new)
        decay = jnp.exp(b_last + m_prev - m_new)
        c_ref[0, h] = decay * c_prev + _tn((g * vh).astype(BF16), kb)
        n_ref[0, h:h + 1, :] = decay * n_prev + jnp.sum(g * kh, axis=0, keepdims=True)
        m_all = jnp.where(lane == h, m_new, m_all)
    m_ref[0] = m_all


def _mlstm_prompt(qk_m, v_m, if_m, if_t, conv_w, chunk):
    b, s, _ = qk_m.shape
    tok = lambda w: pl.BlockSpec((1, chunk, w), lambda bi, i: (bi, i, 0))
    return pl.pallas_call(
        _mlstm_body,
        grid=(b, s // chunk),
        in_specs=[tok(2 * D_MLSTM), tok(D_MLSTM), tok(LANES),
                  pl.BlockSpec((1, N_GATES, chunk), lambda bi, i: (bi, 0, i)),
                  pl.BlockSpec((CONV_W, 2 * D_MLSTM), lambda bi, i: (0, 0))],
        out_specs=[tok(D_MLSTM),
                   pl.BlockSpec((1, MLSTM_HEADS, MLSTM_HEAD_DIM, MLSTM_HEAD_DIM), lambda bi, i: (bi, 0, 0, 0)),
                   pl.BlockSpec((1, MLSTM_HEADS, MLSTM_HEAD_DIM), lambda bi, i: (bi, 0, 0)),
                   pl.BlockSpec((1, 1, LANES), lambda bi, i: (bi, 0, 0))],
        out_shape=[jax.ShapeDtypeStruct((b, s, D_MLSTM), F32),
                   jax.ShapeDtypeStruct((b, MLSTM_HEADS, MLSTM_HEAD_DIM, MLSTM_HEAD_DIM), F32),
                   jax.ShapeDtypeStruct((b, MLSTM_HEADS, MLSTM_HEAD_DIM), F32),
                   jax.ShapeDtypeStruct((b, 1, LANES), F32)],
        scratch_shapes=[pltpu.VMEM((chunk, 2 * D_MLSTM), F32)],
        compiler_params=_cparams(("arbitrary", "arbitrary")),
        name="mlstm_prompt",
    )(qk_m, v_m, if_m, if_t, conv_w)


def _back_body(x_ref, a_ref, hm_ref, om_ref, g1_ref, sc_ref, sh_ref, ag_ref, mg_ref, wo_ref, n2_ref, wq_ref,
               x1_ref, h2_ref, pq_ref):
    a = a_ref[0]
    tm = a.shape[0]
    lane = _iota((tm, LANES), 1)
    low = lane < ATTN_HEAD_DIM
    proj = jnp.zeros((tm, D_MODEL), F32)
    for t in range(D_ATTN // LANES):
        sl = slice(t * LANES, (t + 1) * LANES)
        at = a[:, sl]
        sq = at * at
        s_lo = jnp.sum(jnp.where(low, sq, 0.0), axis=-1, keepdims=True)
        s_hi = jnp.sum(jnp.where(low, 0.0, sq), axis=-1, keepdims=True)
        scale = jnp.where(low, lax.rsqrt(s_lo / ATTN_HEAD_DIM + EPS), lax.rsqrt(s_hi / ATTN_HEAD_DIM + EPS))
        an = at * scale * ag_ref[:, sl]
        proj = proj + jnp.dot(an.astype(BF16), wo_ref[sl, :], preferred_element_type=F32)
    hm = hm_ref[0]
    om = om_ref[0]
    for t in range(MLSTM_HEADS):
        sl = slice(t * LANES, (t + 1) * LANES)
        ht = hm[:, sl]
        hn = ht * lax.rsqrt(jnp.mean(ht * ht, axis=-1, keepdims=True) + EPS) * mg_ref[:, sl]
        mm = jax.nn.sigmoid(om[:, sl]) * hn
        proj = proj + jnp.dot(mm.astype(BF16), wo_ref[D_ATTN + t * LANES:D_ATTN + (t + 1) * LANES, :],
                              preferred_element_type=F32)
    x1 = x_ref[0] + g1_ref[0] * proj
    x1_ref[0] = x1
    h2 = x1 * lax.rsqrt(jnp.mean(x1 * x1, axis=-1, keepdims=True) + EPS) * n2_ref[...]
    h2 = h2 * (1.0 + sc_ref[0]) + sh_ref[0]
    h2_ref[0] = h2
    pq_ref[0] = jnp.dot(h2.astype(BF16), wq_ref[...], preferred_element_type=F32)


def _back(x, attn, h_m, o_m, g1, sc2, sh2, attn_g, mlstm_g, w_out, norm2_g, w_query, tm):
    b, s, d = x.shape
    per_row = g1.shape[1] != 1
    mod_spec = (pl.BlockSpec((1, tm, d), lambda bi, i: (bi, i, 0)) if per_row
                else pl.BlockSpec((1, 1, d), lambda bi, i: (bi, 0, 0)))
    nq = w_query.shape[1]
    const = lambda shape: pl.BlockSpec(shape, lambda bi, i: (0,) * len(shape))
    tok = lambda w: pl.BlockSpec((1, tm, w), lambda bi, i: (bi, i, 0))
    return pl.pallas_call(
        _back_body,
        grid=(b, s // tm),
        in_specs=[tok(d), tok(D_ATTN), tok(D_MLSTM), tok(D_MLSTM), mod_spec, mod_spec, mod_spec,
                  const((1, D_ATTN)), const((1, D_MLSTM)), const((d, d)), const((1, d)), const((d, nq))],
        out_specs=[tok(d), tok(d), tok(nq)],
        out_shape=[jax.ShapeDtypeStruct((b, s, d), F32), jax.ShapeDtypeStruct((b, s, d), F32),
                   jax.ShapeDtypeStruct((b, s, nq), F32)],
        compiler_params=_cparams(("arbitrary", "arbitrary")),
        name="back",
    )(x, attn, h_m, o_m, g1, sc2, sh2, attn_g.reshape(1, D_ATTN), mlstm_g.reshape(1, D_MLSTM),
      w_out.astype(BF16), norm2_g.reshape(1, d), w_query.astype(BF16))


def _peer_route_body(pq_ref, sk_ref, idx_ref, gate_ref):
    tp = pq_ref.shape[0]
    k = PEER_TOPK
    lane = _iota((tp, LANES), 1)
    lane2 = _iota((tp, k * k), 1)
    er = _iota((LANES, k * k), 0)
    ec = _iota((LANES, k * k), 1)
    rep = (ec // k == er).astype(F32)
    til = ((ec & (k - 1)) == er).astype(F32)

    def topk_half(s):
        vals = jnp.zeros((tp, LANES), F32)
        idxs = jnp.zeros((tp, LANES), I32)
        for r in range(k):
            m = jnp.max(s, axis=-1, keepdims=True)
            i = jnp.min(jnp.where(s == m, lane, LANES), axis=-1, keepdims=True)
            vals = jnp.where(lane == r, m, vals)
            idxs = jnp.where(lane == r, i, idxs)
            s = jnp.where(lane == i, -jnp.inf, s)
        return vals, idxs

    c_s = jnp.zeros((tp, LANES), F32)
    c_mx = jnp.zeros((tp, LANES), F32)
    e_id = jnp.zeros((tp, LANES), I32)
    for h in range(PEER_HEADS):
        halves = []
        for p in range(2):
            c0 = (2 * h + p) * PEER_HALF
            qh = pq_ref[:, c0:c0 + PEER_HALF].astype(BF16)
            s = _nt(qh, sk_ref[2 * h + p].astype(BF16))
            halves.append(topk_half(s))
        (v0, i0), (v1, i1) = halves
        cand = (jnp.dot(v0, rep, precision=HI, preferred_element_type=F32)
                + jnp.dot(v1, til, precision=HI, preferred_element_type=F32))
        eid = (jnp.dot(i0.astype(F32), rep, precision=HI, preferred_element_type=F32) * PEER_N_KEYS
               + jnp.dot(i1.astype(F32), til, precision=HI, preferred_element_type=F32)).astype(I32)
        comb = lane2 * (PEER_N_KEYS * PEER_N_KEYS) + eid
        for r in range(k):
            m = jnp.max(cand, axis=-1, keepdims=True)
            key = jnp.min(jnp.where(cand == m, comb, jnp.iinfo(jnp.int32).max), axis=-1, keepdims=True)
            out_lane = lane == h * k + r
            c_s = jnp.where(out_lane, m, c_s)
            e_id = jnp.where(out_lane, key & (PEER_N_KEYS * PEER_N_KEYS - 1), e_id)
            if r == 0:
                c_mx = jnp.where((lane >= h * k) & (lane < (h + 1) * k), m, c_mx)
            cand = jnp.where(comb == key, -jnp.inf, cand)
    e = jnp.exp(c_s - c_mx)
    seg = (_iota((LANES, LANES), 0) // k == _iota((LANES, LANES), 1) // k).astype(F32)
    gate_ref[...] = e / jnp.dot(e, seg, precision=HI, preferred_element_type=F32)
    idx_ref[...] = e_id


def _peer_route(pq, sub_keys, tp):
    n = pq.shape[0]
    sk = sub_keys.reshape(PEER_HEADS * 2, PEER_N_KEYS, PEER_HALF)
    return pl.pallas_call(
        _peer_route_body,
        grid=(n // tp,),
        in_specs=[pl.BlockSpec((tp, pq.shape[1]), lambda i: (i, 0)),
                  pl.BlockSpec(sk.shape, lambda i: (0, 0, 0))],
        out_specs=[pl.BlockSpec((tp, LANES), lambda i: (i, 0)), pl.BlockSpec((tp, LANES), lambda i: (i, 0))],
        out_shape=[jax.ShapeDtypeStruct((n, LANES), I32), jax.ShapeDtypeStruct((n, LANES), F32)],
        compiler_params=_cparams(("arbitrary",)),
        name="peer_route",
    )(pq, sk)


N_SEL = PEER_HEADS * PEER_TOPK


def _peer_expert_body(idx_ref, gate_ref, h2_ref, x1_ref, g2_ref, nf_ref, uv_ref, y_ref, buf, sem, peer_scr):
    tt = h2_ref.shape[1]

    def row_copy(e, slot, k):
        return pltpu.make_async_copy(uv_ref.at[pl.ds(e, 1)], buf.at[slot, pl.ds(k, 1)], sem.at[slot])

    def issue(t, slot):
        def one(k, carry):
            row_copy(idx_ref[0, 0, t * N_SEL + k], slot, k).start()
            return carry
        lax.fori_loop(0, N_SEL, one, 0, unroll=8)

    def wait_all(slot):
        pltpu.make_async_copy(uv_ref.at[pl.ds(0, N_SEL)], buf.at[slot], sem.at[slot]).wait()

    rr = _iota((N_SEL, LANES), 0)
    cc = _iota((N_SEL, LANES), 1)
    diag = rr == cc
    issue(0, 0)

    def token(t, carry):
        slot = t & 1

        @pl.when(t + 1 < tt)
        def _():
            issue(t + 1, 1 - slot)

        wait_all(slot)
        x = h2_ref[0, pl.ds(t, 1), :]
        u = buf[slot, :, 0:D_MODEL]
        v = buf[slot, :, D_MODEL:2 * D_MODEL]
        s = jnp.sum(u * x, axis=-1, keepdims=True)
        act = 0.5 * s * (1.0 + lax.erf(s * (0.5 ** 0.5)))
        g_row = gate_ref[0, pl.ds(t, 1), :]
        g_col = jnp.sum(jnp.where(diag, jnp.broadcast_to(g_row, (N_SEL, LANES)), 0.0), axis=-1, keepdims=True)
        peer_scr[pl.ds(t, 1), :] = jnp.sum(v * (g_col * act), axis=0, keepdims=True)
        return carry

    lax.fori_loop(0, tt, token, 0)
    y = x1_ref[0] + g2_ref[0] * peer_scr[...]
    y_ref[0] = y * lax.rsqrt(jnp.mean(y * y, axis=-1, keepdims=True) + EPS) * nf_ref[...]


def _peer_experts(idx, gate, h2, x1, g2, normf_g, uv, tt):
    b, s, d = h2.shape
    nt = s // tt
    per_row = g2.shape[1] != 1
    mod_spec = (pl.BlockSpec((1, tt, d), lambda bi, i: (bi, i, 0)) if per_row
                else pl.BlockSpec((1, 1, d), lambda bi, i: (bi, 0, 0)))
    tok = lambda w: pl.BlockSpec((1, tt, w), lambda bi, i: (bi, i, 0))
    idx3 = idx.reshape(b * nt, 1, tt * N_SEL)
    return pl.pallas_call(
        _peer_expert_body,
        grid=(b, nt),
        in_specs=[pl.BlockSpec((1, 1, tt * N_SEL), lambda bi, i: (bi * nt + i, 0, 0), memory_space=pltpu.SMEM),
                  tok(LANES), tok(d), tok(d), mod_spec,
                  pl.BlockSpec((1, d), lambda bi, i: (0, 0)),
                  pl.BlockSpec(memory_space=pl.ANY)],
        out_specs=tok(d),
        out_shape=jax.ShapeDtypeStruct((b, s, d), F32),
        scratch_shapes=[pltpu.VMEM((2, N_SEL, 2 * d), F32), pltpu.SemaphoreType.DMA((2,)),
                        pltpu.VMEM((tt, d), F32)],
        compiler_params=_cparams(("arbitrary", "arbitrary")),
        name="peer_experts",
    )(idx3, gate.reshape(b, s, LANES), h2, x1, g2, normf_g.reshape(1, d), uv)


def _layer_tail(x, mods, attn, h_m, o_m, attn_g, mlstm_g, w_out, norm2_g, w_query, sub_keys, uv, normf_g, tm, tp, tt):
    g1, sh2, sc2, g2 = mods
    b, s, d = x.shape
    x1, h2, pq = _back(x, attn, h_m, o_m, g1, sc2, sh2, attn_g, mlstm_g, w_out, norm2_g, w_query, tm)
    idx, gate = _peer_route(pq.reshape(b * s, -1), sub_keys, tp)
    return _peer_experts(idx, gate, h2, x1, g2, normf_g, uv, tt)


def _prompt_path(x, mod6, p):
    b, s, d = x.shape
    sh1, sc1, g1, sh2, sc2, g2 = [m.reshape(b, 1, d) for m in mod6]
    tm = min(512, s)
    q, k, v, qb, kb, vb, qk_m, v_m, o_m, if_m, if_t = _front(x, sc1, sh1, p["norm1_g"], p["w_in"], p["b_if"], tm)
    attn = _moba_prompt(q, k, qb, kb, vb)
    h_m, c_p, n_p, m_p = _mlstm_prompt(qk_m, v_m, if_m, if_t, p["conv_w"], min(256, s))
    y = _layer_tail(x, (g1, sh2, sc2, g2), attn, h_m, o_m, p["attn_norm_g"], p["mlstm_norm_g"], p["w_out"],
                    p["norm2_g"], p["peer_w_query"], p["peer_sub_keys"], p["uv"], p["normf_g"],
                    min(256, s), min(128, b * s), min(64, s))
    conv_p = qk_m[:, s - (CONV_W - 1):, :]
    return y, k, v, c_p, n_p, m_p[:, 0, :MLSTM_HEADS], conv_p


PAGES_PER_STEP = 8


def _page_sum_body(pt_ref, *refs, n_in):
    del pt_ref
    g = pl.program_id(1)
    out_ref = refs[n_in]

    @pl.when(g == 0)
    def _():
        out_ref[...] = jnp.zeros_like(out_ref)

    lane = _iota((D_ATTN, LANES), 1)
    acc = out_ref[0]
    for j in range(n_in):
        page = refs[j][0, 0].reshape(D_ATTN, refs[j].shape[-1])
        acc = jnp.where(lane == g * n_in + j, jnp.sum(page, axis=-1, keepdims=True), acc)
    out_ref[0] = acc


def _page_sums(kp5, pt_flat, layer, db, n_pages):
    assert n_pages <= LANES
    n_in = min(PAGES_PER_STEP, n_pages)
    assert n_pages % n_in == 0
    ps = kp5.shape[-1]

    def page_spec(j):
        return pl.BlockSpec((1, 1, ATTN_HEADS, ATTN_HEAD_DIM, ps),
                            lambda d, g, pt: (layer, pt[d * n_pages + g * n_in + j], 0, 0, 0))

    return pl.pallas_call(
        functools.partial(_page_sum_body, n_in=n_in),
        grid_spec=pltpu.PrefetchScalarGridSpec(
            num_scalar_prefetch=1,
            grid=(db, n_pages // n_in),
            in_specs=[page_spec(j) for j in range(n_in)],
            out_specs=pl.BlockSpec((1, D_ATTN, LANES), lambda d, g, pt: (d, 0, 0)),
        ),
        out_shape=jax.ShapeDtypeStruct((db, D_ATTN, LANES), F32),
        compiler_params=_cparams(("arbitrary", "arbitrary")),
        name="sample_page_sums",
    )(pt_flat, *([kp5] * n_in))


def _sample_select_body(q_ref, ks_ref, o_ref, *, nbp, ppb):
    q = jnp.broadcast_to(q_ref[0], (ATTN_HEADS, D_ATTN))
    qbd = jnp.where(_iota(q.shape, 1) // ATTN_HEAD_DIM == _iota(q.shape, 0), q, 0.0)
    s_page = jnp.dot(qbd, ks_ref[0], precision=HI, preferred_element_type=F32)
    pr = _iota((LANES, LANES), 0)
    pc = _iota((LANES, LANES), 1)
    merge = jnp.where((pr // ppb == pc) & (pc < nbp), 1.0 / MOBA_BLOCK, 0.0)
    s = jnp.dot(s_page, merge, precision=HI, preferred_element_type=F32)
    lane = _iota(s.shape, 1)
    s = jnp.where(lane < nbp, s, -jnp.inf)
    top = jnp.zeros(s.shape, I32)
    for r in range(min(MOBA_TOPK, nbp)):
        m = jnp.max(s, axis=-1, keepdims=True)
        i = jnp.min(jnp.where(s == m, lane, LANES), axis=-1, keepdims=True)
        top = jnp.where(lane == r, i, top)
        s = jnp.where(lane == i, -jnp.inf, s)
    o_ref[0] = top


def _sample_select(q, ksum, nbp, ppb):
    db = q.shape[0]
    return pl.pallas_call(
        functools.partial(_sample_select_body, nbp=nbp, ppb=ppb),
        grid=(db,),
        in_specs=[pl.BlockSpec((1, 1, D_ATTN), lambda d: (d, 0, 0)),
                  pl.BlockSpec((1, D_ATTN, LANES), lambda d: (d, 0, 0))],
        out_specs=pl.BlockSpec((1, ATTN_HEADS, LANES), lambda d: (d, 0, 0)),
        out_shape=jax.ShapeDtypeStruct((db, ATTN_HEADS, LANES), I32),
        compiler_params=_cparams(("arbitrary",)),
        name="sample_select",
    )(q.reshape(db, 1, D_ATTN), ksum)


def _row_to_col(row):
    eye = _iota((LANES, LANES), 0) == _iota((LANES, LANES), 1)
    return jnp.sum(jnp.where(eye, jnp.broadcast_to(row, (LANES, LANES)), 0.0), axis=-1, keepdims=True)


def _col_to_row(col):
    eye = _iota((LANES, LANES), 0) == _iota((LANES, LANES), 1)
    return jnp.sum(jnp.where(eye, jnp.broadcast_to(col, (LANES, LANES)), 0.0), axis=0, keepdims=True)


def _sample_attend_body(pt_ref, top_ref, q_ref, kn_ref, vn_ref, kp_ref, vp_ref, o_ref, kbuf, vbuf, sem,
                        *, layer, n_pages, n_sel, ppb):
    d = pl.program_id(0)
    per_head = n_sel * ppb

    def copies(h, r, j):
        blk = top_ref[(d * ATTN_HEADS + h) * n_sel + r]
        phys = pt_ref[d * n_pages + blk * ppb + j]
        slot = h * per_head + r * ppb + j
        return (pltpu.make_async_copy(kp_ref.at[layer, phys, h], kbuf.at[slot], sem.at[0]),
                pltpu.make_async_copy(vp_ref.at[layer, phys, h], vbuf.at[slot], sem.at[1]))

    todo = [(h, r, j) for h in range(ATTN_HEADS) for r in range(n_sel) for j in range(ppb)]
    for hrj in todo:
        for c in copies(*hrj):
            c.start()
    q_row = q_ref[0] * (ATTN_HEAD_DIM ** -0.5)
    k_row = kn_ref[0]
    v_row = vn_ref[0]
    tiles = D_ATTN // LANES
    q_col = jnp.concatenate([_row_to_col(q_row[:, t * LANES:(t + 1) * LANES]) for t in range(tiles)], axis=0)
    head_of_lane = _iota((1, D_ATTN), 1) // ATTN_HEAD_DIM
    qk_own = q_row * k_row
    for hrj in todo:
        for c in copies(*hrj):
            c.wait()
    acc_cols = []
    p_own_row = jnp.zeros((1, D_ATTN), F32)
    l_row = jnp.ones((1, D_ATTN), F32)
    for h in range(ATTN_HEADS):
        qc = q_col[h * ATTN_HEAD_DIM:(h + 1) * ATTN_HEAD_DIM]
        base = h * per_head
        s_pages = [jnp.sum(kbuf[base + g] * qc, axis=0, keepdims=True) for g in range(per_head)]
        s_own = jnp.sum(jnp.where(head_of_lane == h, qk_own, 0.0), axis=-1, keepdims=True)
        m = s_own
        for s in s_pages:
            m = jnp.maximum(m, jnp.max(s, axis=-1, keepdims=True))
        p_own = jnp.exp(s_own - m)
        l = p_own
        acc = jnp.zeros(kbuf.shape[1:], F32)
        for g, s in enumerate(s_pages):
            p = jnp.exp(s - m)
            l = l + jnp.sum(p, axis=-1, keepdims=True)
            acc = acc + vbuf[base + g] * p
        acc_cols.append(jnp.sum(acc, axis=-1, keepdims=True))
        p_own_row = jnp.where(head_of_lane == h, p_own, p_own_row)
        l_row = jnp.where(head_of_lane == h, l, l_row)
    acc_col = jnp.concatenate(acc_cols, axis=0)
    acc_row = jnp.concatenate([_col_to_row(acc_col[t * LANES:(t + 1) * LANES]) for t in range(tiles)], axis=1)
    o_ref[0] = (acc_row + p_own_row * v_row) / l_row


def _sample_attend(q, k_new, v_new, kp5, vp5, pt_flat, top_flat, layer, n_pages, n_sel, ppb):
    db = q.shape[0]
    ps = kp5.shape[-1]
    row = pl.BlockSpec((1, 1, D_ATTN), lambda d, pt, top: (d, 0, 0))
    n_buf = ATTN_HEADS * n_sel * ppb
    out = pl.pallas_call(
        functools.partial(_sample_attend_body, layer=layer, n_pages=n_pages, n_sel=n_sel, ppb=ppb),
        grid_spec=pltpu.PrefetchScalarGridSpec(
            num_scalar_prefetch=2,
            grid=(db,),
            in_specs=[row, row, row, pl.BlockSpec(memory_space=pl.ANY), pl.BlockSpec(memory_space=pl.ANY)],
            out_specs=row,
            scratch_shapes=[pltpu.VMEM((n_buf, ATTN_HEAD_DIM, ps), F32), pltpu.VMEM((n_buf, ATTN_HEAD_DIM, ps), F32),
                            pltpu.SemaphoreType.DMA((2,))],
        ),
        out_shape=jax.ShapeDtypeStruct((db, 1, D_ATTN), F32),
        compiler_params=_cparams(("arbitrary",)),
        name="sample_attend",
    )(pt_flat, top_flat, q.reshape(db, 1, D_ATTN), k_new.reshape(db, 1, D_ATTN), v_new.reshape(db, 1, D_ATTN),
      kp5, vp5)
    return out.reshape(db, D_ATTN)


def _moba_sample(q, k_new, v_new, k_pool, v_pool, page_table, layer):
    db = q.shape[0]
    ps = k_pool.shape[2]
    n_pages = page_table.shape[1]
    ppb = MOBA_BLOCK // ps
    nbp = (n_pages * ps) // MOBA_BLOCK
    assert nbp >= 1 and (n_pages - nbp * ppb) * ps == 0, "the cached rows must fill whole MoBA blocks"
    n_sel = min(MOBA_TOPK, nbp)
    kp5 = jnp.transpose(k_pool, (0, 1, 3, 4, 2))
    vp5 = jnp.transpose(v_pool, (0, 1, 3, 4, 2))
    pt_flat = page_table.reshape(-1)
    ksum = _page_sums(kp5, pt_flat, layer, db, n_pages)
    top = _sample_select(q, ksum, nbp, ppb)
    top_flat = top[:, :, :n_sel].reshape(-1)
    return _sample_attend(q, k_new, v_new, kp5, vp5, pt_flat, top_flat, layer, n_pages, n_sel, ppb)


def _mlstm_step_body(qk_ref, vm_ref, if_ref, conv_ref, cw_ref, c_ref, n_ref, m_ref,
                     h_ref, c_out, n_out, m_out):
    tb = qk_ref.shape[0]
    w = cw_ref[...]
    lane = _iota((1, LANES), 1)

    def one(r):
        x = qk_ref[pl.ds(r, 1), :]
        cb = conv_ref[pl.ds(r, 1)][0]
        y = cb[0:1] * w[0:1]
        for j in range(1, CONV_W - 1):
            y = y + cb[j:j + 1] * w[j:j + 1]
        y = y + x * w[CONV_W - 1:CONV_W]
        qkc = _silu(y)
        gates = if_ref[pl.ds(r, 1), :]
        m_all = m_ref[pl.ds(r, 1), :]
        v_all = vm_ref[pl.ds(r, 1), :]
        for h in range(MLSTM_HEADS):
            sl = slice(h * MLSTM_HEAD_DIM, (h + 1) * MLSTM_HEAD_DIM)
            q = qkc[:, sl]
            k = qkc[:, D_MLSTM + h * MLSTM_HEAD_DIM:D_MLSTM + (h + 1) * MLSTM_HEAD_DIM] * (MLSTM_HEAD_DIM ** -0.5)
            v = v_all[:, sl]
            ig = gates[:, h:h + 1]
            fg = gates[:, MLSTM_HEADS + h:MLSTM_HEADS + h + 1]
            m_prev = m_all[:, h:h + 1]
            inter = _log_sigmoid(fg) + m_prev
            m_t = jnp.maximum(inter, ig)
            wgt = jnp.exp(ig - m_t)
            w0 = jnp.exp(inter - m_t)
            c_prev = c_ref[pl.ds(r, 1), h][0]
            n_prev = n_ref[pl.ds(r, 1), h]
            sqk = jnp.sum(q * k, axis=-1, keepdims=True) * wgt
            cq = _nt(jnp.broadcast_to(q, (SUBLANES, MLSTM_HEAD_DIM)), c_prev, precision=HI)[0:1]
            num = w0 * cq + sqk * v
            den = w0 * jnp.sum(n_prev * q, axis=-1, keepdims=True) + sqk
            h_ref[pl.ds(r, 1), sl] = num / jnp.maximum(jnp.abs(den), jnp.exp(-m_t))
            c_out[pl.ds(r, 1), h] = (w0 * c_prev + (wgt * _row_to_col(v)) * k)[None]
            n_out[pl.ds(r, 1), h] = w0 * n_prev + wgt * k
            m_all = jnp.where(lane == h, m_t, m_all)
        m_out[pl.ds(r, 1), :] = m_all

    for r in range(tb):
        one(r)


def _mlstm_step(qk_m, v_m, if_m, conv_state, conv_w, c, n, m_pad, tb):
    db = qk_m.shape[0]
    row = lambda w: pl.BlockSpec((tb, w), lambda i: (i, 0))
    c_spec = pl.BlockSpec((tb, MLSTM_HEADS, MLSTM_HEAD_DIM, MLSTM_HEAD_DIM), lambda i: (i, 0, 0, 0))
    n_spec = pl.BlockSpec((tb, MLSTM_HEADS, MLSTM_HEAD_DIM), lambda i: (i, 0, 0))
    return pl.pallas_call(
        _mlstm_step_body,
        grid=(db // tb,),
        in_specs=[row(2 * D_MLSTM), row(D_MLSTM), row(LANES),
                  pl.BlockSpec((tb, CONV_W - 1, 2 * D_MLSTM), lambda i: (i, 0, 0)),
                  pl.BlockSpec((CONV_W, 2 * D_MLSTM), lambda i: (0, 0)), c_spec, n_spec, row(LANES)],
        out_specs=[row(D_MLSTM), c_spec, n_spec, row(LANES)],
        out_shape=[jax.ShapeDtypeStruct((db, D_MLSTM), F32), jax.ShapeDtypeStruct(c.shape, F32),
                   jax.ShapeDtypeStruct(n.shape, F32), jax.ShapeDtypeStruct((db, LANES), F32)],
        compiler_params=_cparams(("arbitrary",)),
        name="mlstm_step",
    )(qk_m, v_m, if_m, conv_state, conv_w, c, n, m_pad)


def _sample_path(x, mod6, p, k_pool, v_pool, page_table, c0, n0, m0, conv0, layer):
    db, t, d = x.shape
    assert t == 1, "one new token per sequence"
    assert db % SUBLANES == 0
    xs = x.reshape(1, db, d)
    sh1, sc1, g1, sh2, sc2, g2 = [m.reshape(1, db, d) for m in mod6]
    q, k, v, _, _, _, qk_m, v_m, o_m, if_m, _ = _front(xs, sc1, sh1, p["norm1_g"], p["w_in"], p["b_if"], db)
    q, k, v, qk_m, v_m, if_m = [a[0] for a in (q, k, v, qk_m, v_m, if_m)]
    attn = _moba_sample(q, k, v, k_pool, v_pool, page_table, layer)
    m_pad = jnp.pad(m0, ((0, 0), (0, LANES - MLSTM_HEADS)))
    h_s, c_s, n_s, m_s = _mlstm_step(qk_m, v_m, if_m, conv0, p["conv_w"], c0, n0, m_pad, SUBLANES)
    y = _layer_tail(xs, (g1, sh2, sc2, g2), attn[None], h_s[None], o_m, p["attn_norm_g"], p["mlstm_norm_g"],
                    p["w_out"], p["norm2_g"], p["peer_w_query"], p["peer_sub_keys"], p["uv"], p["normf_g"],
                    db, db, min(64, db))
    conv_s = jnp.concatenate([conv0[:, 1:], qk_m[:, None, :]], axis=1)
    return y.reshape(db, t, d), k, v, c_s, n_s, m_s[:, :MLSTM_HEADS], conv_s


def kernel(x_prompt, x_sample, cache_k_pool, cache_v_pool, page_table, state_mlstm_C, state_mlstm_n, state_mlstm_m, state_conv, c_prompt, c_sample, w_ada, b_ada, norm1_g, w_in, b_if, conv_w, attn_norm_g, mlstm_norm_g, w_out, norm2_g, peer_w_query, peer_sub_keys, peer_u, peer_v, normf_g):
    b, s, d = x_prompt.shape
    db = x_sample.shape[0]
    l = 0
    p = dict(norm1_g=norm1_g[l], w_in=w_in[l], b_if=b_if[l], conv_w=conv_w[l], attn_norm_g=attn_norm_g[l],
             mlstm_norm_g=mlstm_norm_g[l], w_out=w_out[l], norm2_g=norm2_g[l], peer_w_query=peer_w_query[l],
             peer_sub_keys=peer_sub_keys[l], normf_g=normf_g,
             uv=jnp.concatenate([peer_u[l], peer_v[l]], axis=1))
    rows = b + db
    pad = -rows % SUBLANES
    c_all = jnp.pad(jnp.concatenate([c_prompt, c_sample], axis=0), ((0, pad), (0, 0)))
    mod = _ada(c_all, w_ada[l], b_ada[l])
    mod_p = [mod[:b, i * d:(i + 1) * d] for i in range(6)]
    mod_s = [mod[b:rows, i * d:(i + 1) * d] for i in range(6)]
    y_p, k_p, v_p, c_p, n_p, m_p, conv_p = _prompt_path(x_prompt, mod_p, p)
    y_s, k_s, v_s, c_s, n_s, m_s, conv_s = _sample_path(
        x_sample, mod_s, p, cache_k_pool, cache_v_pool, page_table,
        state_mlstm_C[l], state_mlstm_n[l], state_mlstm_m[l], state_conv[l], l)
    hp = (b, s, ATTN_HEADS, ATTN_HEAD_DIM)
    hs = (db, 1, ATTN_HEADS, ATTN_HEAD_DIM)
    return (y_p, y_s, k_p.reshape(hp)[None], v_p.reshape(hp)[None], k_s.reshape(hs)[None], v_s.reshape(hs)[None],
            c_p[None], n_p[None], m_p[None], conv_p[None], c_s[None], n_s[None], m_s[None], conv_s[None])
```

```python
import functools

import jax
import jax.numpy as jnp
from jax import lax
from jax.experimental import pallas as pl
from jax.experimental.pallas import tpu as pltpu

F32 = jnp.float32
BF16 = jnp.bfloat16
I32 = jnp.int32
HI = lax.Precision.HIGHEST

SUBLANES = 8
LANES = 128
VMEM_LIMIT_BYTES = 56 * 1024 * 1024

D_MODEL = 1024
D_ATTN = 512
D_MLSTM = 512
ATTN_HEADS = 8
ATTN_HEAD_DIM = 64
MLSTM_HEADS = 4
MLSTM_HEAD_DIM = 128
MOBA_BLOCK = 256
MOBA_TOPK = 3
CONV_W = 4
PEER_HEADS = 8
PEER_N_KEYS = 128
PEER_TOPK = 16
PEER_HALF = 128
EPS = 1e-6
N_GATES = 2 * MLSTM_HEADS
C_Q, C_K, C_V, C_QK, C_VM, C_OM, C_IF = 0, 512, 1024, 1536, 2560, 3072, 3584
NEG = -1e30


def _cparams(sem):
    return pltpu.CompilerParams(dimension_semantics=sem, vmem_limit_bytes=VMEM_LIMIT_BYTES)


def _nt(a, b, **kw):
    return lax.dot_general(a, b, (((1,), (1,)), ((), ())), preferred_element_type=F32, **kw)


def _tn(a, b, **kw):
    return lax.dot_general(a, b, (((0,), (0,)), ((), ())), preferred_element_type=F32, **kw)


def _silu(x):
    return x * jax.nn.sigmoid(x)


def _log_sigmoid(x):
    return jnp.minimum(x, 0.0) - jnp.log1p(jnp.exp(-jnp.abs(x)))


def _iota(shape, dim):
    return lax.broadcasted_iota(I32, shape, dim)


def _ada_body(c_ref, w_ref, b_ref, o_ref):
    o_ref[...] = jnp.dot(_silu(c_ref[...]), w_ref[...], precision=HI, preferred_element_type=F32) + b_ref[...]


def _ada(c, w_ada, b_ada):
    rows, d = c.shape
    n = w_ada.shape[1]
    tn = 512
    return pl.pallas_call(
        _ada_body,
        grid=(n // tn,),
        in_specs=[pl.BlockSpec((rows, d), lambda j: (0, 0)),
                  pl.BlockSpec((d, tn), lambda j: (0, j)),
                  pl.BlockSpec((1, tn), lambda j: (0, j))],
        out_specs=pl.BlockSpec((rows, tn), lambda j: (0, j)),
        out_shape=jax.ShapeDtypeStruct((rows, n), F32),
        compiler_params=_cparams(("arbitrary",)),
        name="ada_mod",
    )(c, w_ada, b_ada.reshape(1, n))


def _front_body(x_ref, sc_ref, sh_ref, g_ref, w_ref, wif_ref, wift_ref, bif_ref, bift_ref,
                q_ref, k_ref, v_ref, qb_ref, kb_ref, vb_ref, qk_ref, vm_ref, om_ref, if_ref, ift_ref):
    x = x_ref[0]
    h = x * lax.rsqrt(jnp.mean(x * x, axis=-1, keepdims=True) + EPS) * g_ref[...]
    h = h * (1.0 + sc_ref[0]) + sh_ref[0]
    hb = h.astype(BF16)

    def seg(lo, hi):
        return jnp.dot(hb, w_ref[:, lo:hi], preferred_element_type=F32)

    q = seg(C_Q, C_K)
    q_ref[0] = q
    qb_ref[0] = (q * (ATTN_HEAD_DIM ** -0.5)).astype(BF16)
    k = seg(C_K, C_V)
    k_ref[0] = k
    kb_ref[0] = k.astype(BF16)
    v = seg(C_V, C_QK)
    v_ref[0] = v
    vb_ref[0] = v.astype(BF16)
    qk_ref[0] = seg(C_QK, C_VM)
    vm_ref[0] = seg(C_VM, C_OM)
    om_ref[0] = seg(C_OM, C_IF)
    if_ref[0] = jnp.dot(h, wif_ref[...], precision=HI, preferred_element_type=F32) + bif_ref[...]
    ift_ref[0] = _nt(wift_ref[...], h, precision=HI) + bift_ref[...]


def _front(x, sc, sh, norm_g, w_in, b_if, tm):
    b, s, d = x.shape
    per_row = sc.shape[1] != 1
    mod_spec = (pl.BlockSpec((1, tm, d), lambda bi, i: (bi, i, 0)) if per_row
                else pl.BlockSpec((1, 1, d), lambda bi, i: (bi, 0, 0)))
    w_main = w_in[:, :C_IF].astype(BF16)
    w_if = jnp.pad(w_in[:, C_IF:], ((0, 0), (0, LANES - N_GATES)))
    w_ift = w_in[:, C_IF:].T
    b_row = jnp.pad(b_if, (0, LANES - N_GATES)).reshape(1, LANES)
    b_col = b_if.reshape(N_GATES, 1)
    const = lambda shape: pl.BlockSpec(shape, lambda bi, i: (0,) * len(shape))
    tok = lambda w: pl.BlockSpec((1, tm, w), lambda bi, i: (bi, i, 0))
    out_shapes = [
        jax.ShapeDtypeStruct((b, s, D_ATTN), F32),
        jax.ShapeDtypeStruct((b, s, D_ATTN), F32),
        jax.ShapeDtypeStruct((b, s, D_ATTN), F32),
        jax.ShapeDtypeStruct((b, s, D_ATTN), BF16),
        jax.ShapeDtypeStruct((b, s, D_ATTN), BF16),
        jax.ShapeDtypeStruct((b, s, D_ATTN), BF16),
        jax.ShapeDtypeStruct((b, s, 2 * D_MLSTM), F32),
        jax.ShapeDtypeStruct((b, s, D_MLSTM), F32),
        jax.ShapeDtypeStruct((b, s, D_MLSTM), F32),
        jax.ShapeDtypeStruct((b, s, LANES), F32),
        jax.ShapeDtypeStruct((b, N_GATES, s), F32),
    ]
    out_specs = [tok(D_ATTN)] * 6 + [tok(2 * D_MLSTM), tok(D_MLSTM), tok(D_MLSTM), tok(LANES),
                                     pl.BlockSpec((1, N_GATES, tm), lambda bi, i: (bi, 0, i))]
    return pl.pallas_call(
        _front_body,
        grid=(b, s // tm),
        in_specs=[tok(d), mod_spec, mod_spec, const((1, d)), const((d, C_IF)), const((d, LANES)),
                  const((N_GATES, d)), const((1, LANES)), const((N_GATES, 1))],
        out_specs=out_specs,
        out_shape=out_shapes,
        compiler_params=_cparams(("arbitrary", "arbitrary")),
        name="front",
    )(x, sc, sh, norm_g.reshape(1, d), w_main, w_if, w_ift, b_row, b_col)


def _km_body(k_ref, km_ref, *, nb):
    n = pl.program_id(1)

    @pl.when(n == 0)
    def _():
        km_ref[...] = jnp.zeros_like(km_ref)

    lw = km_ref.shape[2]
    ones = jnp.full((MOBA_BLOCK, lw), 1.0 / MOBA_BLOCK, F32)
    mean_b = _tn(k_ref[0], ones, precision=HI)
    row = _iota((D_ATTN, lw), 0)
    col = _iota((D_ATTN, lw), 1)
    km_ref[0] += jnp.where(col == (row // ATTN_HEAD_DIM) * nb + n, mean_b, 0.0)


def _block_means(k, nb, lw):
    b = k.shape[0]
    return pl.pallas_call(
        functools.partial(_km_body, nb=nb),
        grid=(b, nb),
        in_specs=[pl.BlockSpec((1, MOBA_BLOCK, D_ATTN), lambda bi, n: (bi, n, 0))],
        out_specs=pl.BlockSpec((1, D_ATTN, lw), lambda bi, n: (bi, 0, 0)),
        out_shape=jax.ShapeDtypeStruct((b, D_ATTN, lw), F32),
        compiler_params=_cparams(("arbitrary", "arbitrary")),
        name="moba_block_means",
    )(k)


def _sel_body(q_ref, km_ref, o_ref, *, nb):
    own = pl.program_id(1)
    s = jnp.dot(q_ref[0], km_ref[0], precision=HI, preferred_element_type=F32)
    shape = s.shape
    lw = shape[1]
    lane = _iota(shape, 1)
    blk = lane & (nb - 1)
    head = lane >> (nb.bit_length() - 1)
    valid = (blk < own) & (head < ATTN_HEADS)
    sel = jnp.zeros(shape, jnp.bool_)
    for h in range(ATTN_HEADS):
        sm = jnp.where(valid & (head == h), s, -jnp.inf)
        for _ in range(min(MOBA_TOPK, nb)):
            m = jnp.max(sm, axis=-1, keepdims=True)
            i = jnp.min(jnp.where(sm == m, lane, lw), axis=-1, keepdims=True)
            hit = lane == i
            sel = sel | (hit & (m > -jnp.inf))
            sm = jnp.where(hit, -jnp.inf, sm)
    o_ref[0] = jnp.where(sel, 0.0, NEG)


def _moba_select(q, km, nb):
    b, s, _ = q.shape
    lw = km.shape[2]
    return pl.pallas_call(
        functools.partial(_sel_body, nb=nb),
        grid=(b, s // MOBA_BLOCK),
        in_specs=[pl.BlockSpec((1, MOBA_BLOCK, D_ATTN), lambda bi, i: (bi, i, 0)),
                  pl.BlockSpec((1, D_ATTN, lw), lambda bi, i: (bi, 0, 0))],
        out_specs=pl.BlockSpec((1, MOBA_BLOCK, lw), lambda bi, i: (bi, i, 0)),
        out_shape=jax.ShapeDtypeStruct((b, s, lw), F32),
        compiler_params=_cparams(("arbitrary", "arbitrary")),
        name="moba_select",
    )(q, km)


def _attn_body(q_ref, k_ref, v_ref, sb_ref, o_ref, *, nb):
    group = pl.program_id(1)
    own = pl.program_id(2)
    tq = MOBA_BLOCK
    tiles = q_ref.shape[2] // LANES
    n_heads = 2 * tiles
    lane = _iota((tq, LANES), 1)
    low = lane < ATTN_HEAD_DIM
    qs = []
    for t in range(tiles):
        q = q_ref[0, :, t * LANES:(t + 1) * LANES]
        zero = jnp.zeros_like(q)
        qs += [jnp.where(low, q, zero), jnp.where(low, zero, q)]
    sb = sb_ref[0]
    sb_lane = _iota(sb.shape, 1)
    row = _iota((tq, tq), 0)
    col = _iota((tq, tq), 1)

    def attend(j, state, bias_fn):
        start = pl.multiple_of(j * tq, tq)
        new = []
        for hh in range(n_heads):
            sl = slice((hh // 2) * LANES, (hh // 2 + 1) * LANES)
            kj = k_ref[0, pl.ds(start, tq), sl]
            vj = v_ref[0, pl.ds(start, tq), sl]
            m, l, acc = state[hh]
            s = bias_fn(hh, _nt(qs[hh], kj))
            m_new = jnp.maximum(m, jnp.max(s, axis=-1, keepdims=True))
            alpha = jnp.exp(m - m_new)
            p = jnp.exp(s - m_new)
            l = alpha * l + jnp.sum(p, axis=-1, keepdims=True)
            acc = alpha * acc + jnp.dot(p.astype(BF16), vj, preferred_element_type=F32)
            new.append((m_new, l, acc))
        return tuple(new)

    def causal(hh, s):
        return jnp.where(col <= row, s, -jnp.inf)

    init = tuple((jnp.full((tq, 1), -jnp.inf, F32), jnp.zeros((tq, 1), F32), jnp.zeros((tq, LANES), F32))
                 for _ in range(n_heads))
    state = attend(own, init, causal)

    def past(j, state):
        def biased(hh, s):
            c = (n_heads * group + hh) * nb + j
            bias = jnp.max(jnp.where(sb_lane == c, sb, -jnp.inf), axis=-1, keepdims=True)
            return s + bias
        return attend(j, state, biased)

    state = lax.fori_loop(0, own, past, state)
    for t in range(tiles):
        (_, l0, a0), (_, l1, a1) = state[2 * t], state[2 * t + 1]
        o_ref[0, :, t * LANES:(t + 1) * LANES] = jnp.where(low, a0 / l0, a1 / l1)


ATTN_LANES_PER_STEP = 2 * LANES


def _moba_attend(qb, kb, vb, selbias, nb):
    b, s, _ = qb.shape
    lw = selbias.shape[2]
    gw = ATTN_LANES_PER_STEP
    pairs = D_ATTN // gw
    return pl.pallas_call(
        functools.partial(_attn_body, nb=nb),
        grid=(b, pairs, s // MOBA_BLOCK),
        in_specs=[pl.BlockSpec((1, MOBA_BLOCK, gw), lambda bi, p, i: (bi, i, p)),
                  pl.BlockSpec((1, s, gw), lambda bi, p, i: (bi, 0, p)),
                  pl.BlockSpec((1, s, gw), lambda bi, p, i: (bi, 0, p)),
                  pl.BlockSpec((1, MOBA_BLOCK, lw), lambda bi, p, i: (bi, i, 0))],
        out_specs=pl.BlockSpec((1, MOBA_BLOCK, gw), lambda bi, p, i: (bi, i, p)),
        out_shape=jax.ShapeDtypeStruct((b, s, D_ATTN), F32),
        compiler_params=_cparams(("arbitrary", "arbitrary", "arbitrary")),
        name="moba_attend",
    )(qb, kb, vb, selbias)


def _moba_prompt(q, k, qb, kb, vb):
    s = q.shape[1]
    assert s % MOBA_BLOCK == 0
    nb = s // MOBA_BLOCK
    assert nb & (nb - 1) == 0, "block count must be a power of two"
    lw = -(-ATTN_HEADS * nb // LANES) * LANES
    km = _block_means(k, nb, lw)
    selbias = _moba_select(q, km, nb)
    return _moba_attend(qb, kb, vb, selbias, nb)


def _mlstm_body(qk_ref, vm_ref, if_ref, ift_ref, cw_ref, h_ref, c_ref, n_ref, m_ref, prev_scr):
    i = pl.program_id(1)
    L = qk_ref.shape[1]

    @pl.when(i == 0)
    def _():
        prev_scr[...] = jnp.zeros_like(prev_scr)
        c_ref[...] = jnp.zeros_like(c_ref)
        n_ref[...] = jnp.zeros_like(n_ref)
        m_ref[...] = jnp.zeros_like(m_ref)

    x = qk_ref[0]
    prev = prev_scr[...]
    w = cw_ref[...]
    rows = _iota(x.shape, 0)
    y = x * w[CONV_W - 1:CONV_W]
    for sft in range(1, CONV_W):
        xs = jnp.where(rows >= sft, pltpu.roll(x, sft, 0), pltpu.roll(prev, sft, 0))
        y = y + xs * w[CONV_W - 1 - sft:CONV_W - sft]
    prev_scr[...] = x
    qkc = _silu(y)

    gcol = if_ref[0]
    grow = ift_ref[0]
    tri_r = _iota((L, L), 0)
    tri_c = _iota((L, L), 1)
    lower = tri_c <= tri_r
    b_col = jnp.dot(lower.astype(F32), _log_sigmoid(gcol), precision=HI, preferred_element_type=F32)
    b_row = jnp.dot(_log_sigmoid(grow), (tri_r <= tri_c).astype(F32), precision=HI, preferred_element_type=F32)
    m_all = m_ref[0]
    lane = _iota(m_all.shape, 1)
    for h in range(MLSTM_HEADS):
        sl = slice(h * MLSTM_HEAD_DIM, (h + 1) * MLSTM_HEAD_DIM)
        qh = qkc[:, sl]
        kh = qkc[:, D_MLSTM + h * MLSTM_HEAD_DIM:D_MLSTM + (h + 1) * MLSTM_HEAD_DIM] * (MLSTM_HEAD_DIM ** -0.5)
        vh = vm_ref[0][:, sl]
        fh = MLSTM_HEADS + h
        bc = b_col[:, fh:fh + 1]
        br = b_row[fh:fh + 1, :]
        ic = gcol[:, h:h + 1]
        ir = grow[h:h + 1, :]
        m_prev = m_all[:, h:h + 1]
        c_prev = c_ref[0, h]
        n_prev = n_ref[0, h:h + 1, :]
        dmat = jnp.where(lower, bc - br + ir, -jnp.inf)
        inter = bc + m_prev
        m_t = jnp.maximum(inter, jnp.max(dmat, axis=-1, keepdims=True))
        wmat = jnp.exp(dmat - m_t)
        w0 = jnp.exp(inter - m_t)
        qb, kb, vb = qh.astype(BF16), kh.astype(BF16), vh.astype(BF16)
        sqk = _nt(qb, kb) * wmat
        num = w0 * _nt(qb, c_prev.astype(BF16)) + jnp.dot(sqk.astype(BF16), vb, preferred_element_type=F32)
        den = w0 * jnp.sum(qh * n_prev, axis=-1, keepdims=True) + jnp.sum(sqk, axis=-1, keepdims=True)
        h_ref[0, :, sl] = num / jnp.maximum(jnp.abs(den), jnp.exp(-m_t))
        m_new = m_t[L - 1:L, :]
        b_last = bc[L - 1:L, :]
        g = jnp.exp(b_last - bc + ic - m_new)
        decay = jnp.exp(b_last + m_prev - m_new)
        c_ref[0, h] = decay * c_prev + _tn((g * vh).astype(BF16), kb)
        n_ref[0, h:h + 1, :] = decay * n_prev + jnp.sum(g * kh, axis=0, keepdims=True)
        m_all = jnp.where(lane == h, m_new, m_all)
    m_ref[0] = m_all


def _mlstm_prompt(qk_m, v_m, if_m, if_t, conv_w, chunk):
    b, s, _ = qk_m.shape
    tok = lambda w: pl.BlockSpec((1, chunk, w), lambda bi, i: (bi, i, 0))
    return pl.pallas_call(
        _mlstm_body,
        grid=(b, s // chunk),
        in_specs=[tok(2 * D_MLSTM), tok(D_MLSTM), tok(LANES),
                  pl.BlockSpec((1, N_GATES, chunk), lambda bi, i: (bi, 0, i)),
                  pl.BlockSpec((CONV_W, 2 * D_MLSTM), lambda bi, i: (0, 0))],
        out_specs=[tok(D_MLSTM),
                   pl.BlockSpec((1, MLSTM_HEADS, MLSTM_HEAD_DIM, MLSTM_HEAD_DIM), lambda bi, i: (bi, 0, 0, 0)),
                   pl.BlockSpec((1, MLSTM_HEADS, MLSTM_HEAD_DIM), lambda bi, i: (bi, 0, 0)),
                   pl.BlockSpec((1, 1, LANES), lambda bi, i: (bi, 0, 0))],
        out_shape=[jax.ShapeDtypeStruct((b, s, D_MLSTM), F32),
                   jax.ShapeDtypeStruct((b, MLSTM_HEADS, MLSTM_HEAD_DIM, MLSTM_HEAD_DIM), F32),
                   jax.ShapeDtypeStruct((b, MLSTM_HEADS, MLSTM_HEAD_DIM), F32),
                   jax.ShapeDtypeStruct((b, 1, LANES), F32)],
        scratch_shapes=[pltpu.VMEM((chunk, 2 * D_MLSTM), F32)],
        compiler_params=_cparams(("arbitrary", "arbitrary")),
        name="mlstm_prompt",
    )(qk_m, v_m, if_m, if_t, conv_w)


def _back_body(x_ref, a_ref, hm_ref, om_ref, g1_ref, sc_ref, sh_ref, ag_ref, mg_ref, wo_ref, n2_ref, wq_ref,
               x1_ref, h2_ref, pq_ref):
    a = a_ref[0]
    tm = a.shape[0]
    lane = _iota((tm, LANES), 1)
    low = lane < ATTN_HEAD_DIM
    proj = jnp.zeros((tm, D_MODEL), F32)
    for t in range(D_ATTN // LANES):
        sl = slice(t * LANES, (t + 1) * LANES)
        at = a[:, sl]
        sq = at * at
        s_lo = jnp.sum(jnp.where(low, sq, 0.0), axis=-1, keepdims=True)
        s_hi = jnp.sum(jnp.where(low, 0.0, sq), axis=-1, keepdims=True)
        scale = jnp.where(low, lax.rsqrt(s_lo / ATTN_HEAD_DIM + EPS), lax.rsqrt(s_hi / ATTN_HEAD_DIM + EPS))
        an = at * scale * ag_ref[:, sl]
        proj = proj + jnp.dot(an.astype(BF16), wo_ref[sl, :], preferred_element_type=F32)
    hm = hm_ref[0]
    om = om_ref[0]
    for t in range(MLSTM_HEADS):
        sl = slice(t * LANES, (t + 1) * LANES)
        ht = hm[:, sl]
        hn = ht * lax.rsqrt(jnp.mean(ht * ht, axis=-1, keepdims=True) + EPS) * mg_ref[:, sl]
        mm = jax.nn.sigmoid(om[:, sl]) * hn
        proj = proj + jnp.dot(mm.astype(BF16), wo_ref[D_ATTN + t * LANES:D_ATTN + (t + 1) * LANES, :],
                              preferred_element_type=F32)
    x1 = x_ref[0] + g1_ref[0] * proj
    x1_ref[0] = x1
    h2 = x1 * lax.rsqrt(jnp.mean(x1 * x1, axis=-1, keepdims=True) + EPS) * n2_ref[...]
    h2 = h2 * (1.0 + sc_ref[0]) + sh_ref[0]
    h2_ref[0] = h2
    pq_ref[0] = jnp.dot(h2.astype(BF16), wq_ref[...], preferred_element_type=F32)


def _back(x, attn, h_m, o_m, g1, sc2, sh2, attn_g, mlstm_g, w_out, norm2_g, w_query, tm):
    b, s, d = x.shape
    per_row = g1.shape[1] != 1
    mod_spec = (pl.BlockSpec((1, tm, d), lambda bi, i: (bi, i, 0)) if per_row
                else pl.BlockSpec((1, 1, d), lambda bi, i: (bi, 0, 0)))
    nq = w_query.shape[1]
    const = lambda shape: pl.BlockSpec(shape, lambda bi, i: (0,) * len(shape))
    tok = lambda w: pl.BlockSpec((1, tm, w), lambda bi, i: (bi, i, 0))
    return pl.pallas_call(
        _back_body,
        grid=(b, s // tm),
        in_specs=[tok(d), tok(D_ATTN), tok(D_MLSTM), tok(D_MLSTM), mod_spec, mod_spec, mod_spec,
                  const((1, D_ATTN)), const((1, D_MLSTM)), const((d, d)), const((1, d)), const((d, nq))],
        out_specs=[tok(d), tok(d), tok(nq)],
        out_shape=[jax.ShapeDtypeStruct((b, s, d), F32), jax.ShapeDtypeStruct((b, s, d), F32),
                   jax.ShapeDtypeStruct((b, s, nq), F32)],
        compiler_params=_cparams(("arbitrary", "arbitrary")),
        name="back",
    )(x, attn, h_m, o_m, g1, sc2, sh2, attn_g.reshape(1, D_ATTN), mlstm_g.reshape(1, D_MLSTM),
      w_out.astype(BF16), norm2_g.reshape(1, d), w_query.astype(BF16))


def _peer_route_body(pq_ref, sk_ref, idx_ref, gate_ref):
    tp = pq_ref.shape[0]
    k = PEER_TOPK
    n_exp = PEER_N_KEYS * PEER_N_KEYS
    out_row = _iota((k, tp), 0)

    def extract(score, tag, fill):
        vals = jnp.zeros((k, tp), F32)
        tags = jnp.zeros((k, tp), F32)
        for r in range(k):
            m = jnp.max(score, axis=0, keepdims=True)
            t = jnp.min(jnp.where(score == m, tag, fill), axis=0, keepdims=True)
            vals = jnp.where(out_row == r, m, vals)
            tags = jnp.where(out_row == r, t, tags)
            score = jnp.where(tag == t, -jnp.inf, score)
        return vals, tags

    key_row = _iota((PEER_N_KEYS, tp), 0).astype(F32)
    pos = _iota((k * k, tp), 0).astype(F32)
    gates, ids = [], []
    for h in range(PEER_HEADS):
        halves = []
        for p in range(2):
            c0 = (2 * h + p) * PEER_HALF
            qh = pq_ref[:, c0:c0 + PEER_HALF].astype(BF16)
            s = _nt(sk_ref[2 * h + p].astype(BF16), qh)
            halves.append(extract(s, key_row, float(PEER_N_KEYS)))
        (v0, i0), (v1, i1) = halves
        cand = jnp.concatenate([v0[a:a + 1] + v1 for a in range(k)], axis=0)
        eid = jnp.concatenate([i0[a:a + 1] * PEER_N_KEYS + i1 for a in range(k)], axis=0)
        comb = pos * n_exp + eid
        c_s, c_key = extract(cand, comb, float(k * k * n_exp))
        e = jnp.exp(c_s - c_s[0:1])
        gates.append(e / jnp.sum(e, axis=0, keepdims=True))
        ids.append(c_key)
    gate_ref[...] = jnp.concatenate(gates, axis=0).T
    keys = jnp.concatenate(ids, axis=0).T.astype(I32)
    idx_ref[...] = keys & (n_exp - 1)


def _peer_route(pq, sub_keys):
    n = pq.shape[0]
    tp = LANES
    n_pad = -(-n // tp) * tp
    pq = jnp.pad(pq, ((0, n_pad - n), (0, 0)))
    sk = sub_keys.reshape(PEER_HEADS * 2, PEER_N_KEYS, PEER_HALF)
    idx, gate = pl.pallas_call(
        _peer_route_body,
        grid=(n_pad // tp,),
        in_specs=[pl.BlockSpec((tp, pq.shape[1]), lambda i: (i, 0)),
                  pl.BlockSpec(sk.shape, lambda i: (0, 0, 0))],
        out_specs=[pl.BlockSpec((tp, LANES), lambda i: (i, 0)), pl.BlockSpec((tp, LANES), lambda i: (i, 0))],
        out_shape=[jax.ShapeDtypeStruct((n_pad, LANES), I32), jax.ShapeDtypeStruct((n_pad, LANES), F32)],
        compiler_params=_cparams(("arbitrary",)),
        name="peer_route",
    )(pq, sk)
    return idx[:n], gate[:n]


N_SEL = PEER_HEADS * PEER_TOPK


def _peer_expert_body(idx_ref, gate_ref, h2_ref, x1_ref, g2_ref, nf_ref, uv_ref, y_ref, buf, sem, peer_scr):
    tt = h2_ref.shape[1]

    def issue(t, slot):
        for k in range(N_SEL):
            e = idx_ref[0, 0, t * N_SEL + k]
            pltpu.make_async_copy(uv_ref.at[e], buf.at[slot, pl.ds(k, 1)], sem.at[slot]).start(priority=k % 2)

    def wait_all(slot):
        pltpu.make_async_copy(uv_ref.at[pl.ds(0, N_SEL), 0], buf.at[slot], sem.at[slot]).wait()

    rr = _iota((N_SEL, LANES), 0)
    cc = _iota((N_SEL, LANES), 1)
    diag = rr == cc

    def compute(t, slot):
        x = h2_ref[0, pl.ds(t, 1), :]
        u = buf[slot, :, 0:D_MODEL]
        v = buf[slot, :, D_MODEL:2 * D_MODEL]
        s = jnp.sum(u * x, axis=-1, keepdims=True)
        act = 0.5 * s * (1.0 + lax.erf(s * (0.5 ** 0.5)))
        g_row = gate_ref[0, pl.ds(t, 1), :]
        g_col = jnp.sum(jnp.where(diag, jnp.broadcast_to(g_row, (N_SEL, LANES)), 0.0), axis=-1, keepdims=True)
        peer_scr[pl.ds(t, 1), :] = jnp.sum(v * (g_col * act), axis=0, keepdims=True)

    issue(0, 0)

    def token_pair(i, carry):
        t0 = 2 * i
        issue(t0 + 1, 1)
        wait_all(0)
        compute(t0, 0)

        @pl.when(t0 + 2 < tt)
        def _():
            issue(t0 + 2, 0)

        wait_all(1)
        compute(t0 + 1, 1)
        return carry

    lax.fori_loop(0, tt // 2, token_pair, 0)
    y = x1_ref[0] + g2_ref[0] * peer_scr[...]
    y_ref[0] = y * lax.rsqrt(jnp.mean(y * y, axis=-1, keepdims=True) + EPS) * nf_ref[...]


def _peer_experts(idx, gate, h2, x1, g2, normf_g, uv, tt):
    b, s, d = h2.shape
    assert tt % 2 == 0 and s % tt == 0
    nt = s // tt
    per_row = g2.shape[1] != 1
    mod_spec = (pl.BlockSpec((1, tt, d), lambda bi, i: (bi, i, 0)) if per_row
                else pl.BlockSpec((1, 1, d), lambda bi, i: (bi, 0, 0)))
    tok = lambda w: pl.BlockSpec((1, tt, w), lambda bi, i: (bi, i, 0))
    idx3 = idx.reshape(b * nt, 1, tt * N_SEL)
    return pl.pallas_call(
        _peer_expert_body,
        grid=(b, nt),
        in_specs=[pl.BlockSpec((1, 1, tt * N_SEL), lambda bi, i: (bi * nt + i, 0, 0), memory_space=pltpu.SMEM),
                  tok(LANES), tok(d), tok(d), mod_spec,
                  pl.BlockSpec((1, d), lambda bi, i: (0, 0)),
                  pl.BlockSpec(memory_space=pl.ANY)],
        out_specs=tok(d),
        out_shape=jax.ShapeDtypeStruct((b, s, d), F32),
        scratch_shapes=[pltpu.VMEM((2, N_SEL, 2 * d), F32), pltpu.SemaphoreType.DMA((2,)),
                        pltpu.VMEM((tt, d), F32)],
        compiler_params=_cparams(("arbitrary", "arbitrary")),
        name="peer_experts",
    )(idx3, gate.reshape(b, s, LANES), h2, x1, g2, normf_g.reshape(1, d), uv)


def _layer_tail(x, mods, attn, h_m, o_m, attn_g, mlstm_g, w_out, norm2_g, w_query, sub_keys, uv, normf_g, tm, tt):
    g1, sh2, sc2, g2 = mods
    b, s, d = x.shape
    x1, h2, pq = _back(x, attn, h_m, o_m, g1, sc2, sh2, attn_g, mlstm_g, w_out, norm2_g, w_query, tm)
    idx, gate = _peer_route(pq.reshape(b * s, -1), sub_keys)
    return _peer_experts(idx, gate, h2, x1, g2, normf_g, uv, tt)


def _prompt_path(x, mod6, p):
    b, s, d = x.shape
    sh1, sc1, g1, sh2, sc2, g2 = [m.reshape(b, 1, d) for m in mod6]
    tm = min(512, s)
    q, k, v, qb, kb, vb, qk_m, v_m, o_m, if_m, if_t = _front(x, sc1, sh1, p["norm1_g"], p["w_in"], p["b_if"], tm)
    attn = _moba_prompt(q, k, qb, kb, vb)
    h_m, c_p, n_p, m_p = _mlstm_prompt(qk_m, v_m, if_m, if_t, p["conv_w"], min(256, s))
    y = _layer_tail(x, (g1, sh2, sc2, g2), attn, h_m, o_m, p["attn_norm_g"], p["mlstm_norm_g"], p["w_out"],
                    p["norm2_g"], p["peer_w_query"], p["peer_sub_keys"], p["uv"], p["normf_g"],
                    min(256, s), min(64, s))
    conv_p = qk_m[:, s - (CONV_W - 1):, :]
    return y, k, v, c_p, n_p, m_p[:, 0, :MLSTM_HEADS], conv_p


PAGES_PER_STEP = 8


def _page_sum_body(pt_ref, *refs, n_in):
    del pt_ref
    g = pl.program_id(1)
    out_ref = refs[n_in]

    @pl.when(g == 0)
    def _():
        out_ref[...] = jnp.zeros_like(out_ref)

    lane = _iota((D_ATTN, LANES), 1)
    acc = out_ref[0]
    for j in range(n_in):
        page = refs[j][0, 0].reshape(D_ATTN, refs[j].shape[-1])
        acc = jnp.where(lane == g * n_in + j, jnp.sum(page, axis=-1, keepdims=True), acc)
    out_ref[0] = acc


def _page_sums(kp5, pt_flat, layer, db, n_pages):
    assert n_pages <= LANES
    n_in = min(PAGES_PER_STEP, n_pages)
    assert n_pages % n_in == 0
    ps = kp5.shape[-1]

    def page_spec(j):
        return pl.BlockSpec((1, 1, ATTN_HEADS, ATTN_HEAD_DIM, ps),
                            lambda d, g, pt: (layer, pt[d * n_pages + g * n_in + j], 0, 0, 0))

    return pl.pallas_call(
        functools.partial(_page_sum_body, n_in=n_in),
        grid_spec=pltpu.PrefetchScalarGridSpec(
            num_scalar_prefetch=1,
            grid=(db, n_pages // n_in),
            in_specs=[page_spec(j) for j in range(n_in)],
            out_specs=pl.BlockSpec((1, D_ATTN, LANES), lambda d, g, pt: (d, 0, 0)),
        ),
        out_shape=jax.ShapeDtypeStruct((db, D_ATTN, LANES), F32),
        compiler_params=_cparams(("arbitrary", "arbitrary")),
        name="sample_page_sums",
    )(pt_flat, *([kp5] * n_in))


def _sample_select_body(q_ref, ks_ref, o_ref, *, nbp, ppb):
    q = jnp.broadcast_to(q_ref[0], (ATTN_HEADS, D_ATTN))
    qbd = jnp.where(_iota(q.shape, 1) // ATTN_HEAD_DIM == _iota(q.shape, 0), q, 0.0)
    s_page = jnp.dot(qbd, ks_ref[0], precision=HI, preferred_element_type=F32)
    pr = _iota((LANES, LANES), 0)
    pc = _iota((LANES, LANES), 1)
    merge = jnp.where((pr // ppb == pc) & (pc < nbp), 1.0 / MOBA_BLOCK, 0.0)
    s = jnp.dot(s_page, merge, precision=HI, preferred_element_type=F32)
    lane = _iota(s.shape, 1)
    s = jnp.where(lane < nbp, s, -jnp.inf)
    top = jnp.zeros(s.shape, I32)
    for r in range(min(MOBA_TOPK, nbp)):
        m = jnp.max(s, axis=-1, keepdims=True)
        i = jnp.min(jnp.where(s == m, lane, LANES), axis=-1, keepdims=True)
        top = jnp.where(lane == r, i, top)
        s = jnp.where(lane == i, -jnp.inf, s)
    o_ref[0] = top


def _sample_select(q, ksum, nbp, ppb):
    db = q.shape[0]
    return pl.pallas_call(
        functools.partial(_sample_select_body, nbp=nbp, ppb=ppb),
        grid=(db,),
        in_specs=[pl.BlockSpec((1, 1, D_ATTN), lambda d: (d, 0, 0)),
                  pl.BlockSpec((1, D_ATTN, LANES), lambda d: (d, 0, 0))],
        out_specs=pl.BlockSpec((1, ATTN_HEADS, LANES), lambda d: (d, 0, 0)),
        out_shape=jax.ShapeDtypeStruct((db, ATTN_HEADS, LANES), I32),
        compiler_params=_cparams(("arbitrary",)),
        name="sample_select",
    )(q.reshape(db, 1, D_ATTN), ksum)


def _row_to_col(row):
    eye = _iota((LANES, LANES), 0) == _iota((LANES, LANES), 1)
    return jnp.sum(jnp.where(eye, jnp.broadcast_to(row, (LANES, LANES)), 0.0), axis=-1, keepdims=True)


def _col_to_row(col):
    eye = _iota((LANES, LANES), 0) == _iota((LANES, LANES), 1)
    return jnp.sum(jnp.where(eye, jnp.broadcast_to(col, (LANES, LANES)), 0.0), axis=0, keepdims=True)


def _sample_attend_body(pt_ref, top_ref, q_ref, kn_ref, vn_ref, kp_ref, vp_ref, o_ref, kbuf, vbuf, sem,
                        *, layer, n_pages, n_sel, ppb):
    d = pl.program_id(0)
    per_head = n_sel * ppb

    def copies(h, r, j):
        blk = top_ref[(d * ATTN_HEADS + h) * n_sel + r]
        phys = pt_ref[d * n_pages + blk * ppb + j]
        slot = h * per_head + r * ppb + j
        return (pltpu.make_async_copy(kp_ref.at[layer, phys, h], kbuf.at[slot], sem.at[0]),
                pltpu.make_async_copy(vp_ref.at[layer, phys, h], vbuf.at[slot], sem.at[1]))

    todo = [(h, r, j) for h in range(ATTN_HEADS) for r in range(n_sel) for j in range(ppb)]
    for hrj in todo:
        for c in copies(*hrj):
            c.start()
    q_row = q_ref[0] * (ATTN_HEAD_DIM ** -0.5)
    k_row = kn_ref[0]
    v_row = vn_ref[0]
    tiles = D_ATTN // LANES
    q_col = jnp.concatenate([_row_to_col(q_row[:, t * LANES:(t + 1) * LANES]) for t in range(tiles)], axis=0)
    head_of_lane = _iota((1, D_ATTN), 1) // ATTN_HEAD_DIM
    qk_own = q_row * k_row
    for hrj in todo:
        for c in copies(*hrj):
            c.wait()
    acc_cols = []
    p_own_row = jnp.zeros((1, D_ATTN), F32)
    l_row = jnp.ones((1, D_ATTN), F32)
    for h in range(ATTN_HEADS):
        qc = q_col[h * ATTN_HEAD_DIM:(h + 1) * ATTN_HEAD_DIM]
        base = h * per_head
        s_pages = [jnp.sum(kbuf[base + g] * qc, axis=0, keepdims=True) for g in range(per_head)]
        s_own = jnp.sum(jnp.where(head_of_lane == h, qk_own, 0.0), axis=-1, keepdims=True)
        m = s_own
        for s in s_pages:
            m = jnp.maximum(m, jnp.max(s, axis=-1, keepdims=True))
        p_own = jnp.exp(s_own - m)
        l = p_own
        acc = jnp.zeros(kbuf.shape[1:], F32)
        for g, s in enumerate(s_pages):
            p = jnp.exp(s - m)
            l = l + jnp.sum(p, axis=-1, keepdims=True)
            acc = acc + vbuf[base + g] * p
        acc_cols.append(jnp.sum(acc, axis=-1, keepdims=True))
        p_own_row = jnp.where(head_of_lane == h, p_own, p_own_row)
        l_row = jnp.where(head_of_lane == h, l, l_row)
    acc_col = jnp.concatenate(acc_cols, axis=0)
    acc_row = jnp.concatenate([_col_to_row(acc_col[t * LANES:(t + 1) * LANES]) for t in range(tiles)], axis=1)
    o_ref[0] = (acc_row + p_own_row * v_row) / l_row


def _sample_attend(q, k_new, v_new, kp5, vp5, pt_flat, top_flat, layer, n_pages, n_sel, ppb):
    db = q.shape[0]
    ps = kp5.shape[-1]
    row = pl.BlockSpec((1, 1, D_ATTN), lambda d, pt, top: (d, 0, 0))
    n_buf = ATTN_HEADS * n_sel * ppb
    out = pl.pallas_call(
        functools.partial(_sample_attend_body, layer=layer, n_pages=n_pages, n_sel=n_sel, ppb=ppb),
        grid_spec=pltpu.PrefetchScalarGridSpec(
            num_scalar_prefetch=2,
            grid=(db,),
            in_specs=[row, row, row, pl.BlockSpec(memory_space=pl.ANY), pl.BlockSpec(memory_space=pl.ANY)],
            out_specs=row,
            scratch_shapes=[pltpu.VMEM((n_buf, ATTN_HEAD_DIM, ps), F32), pltpu.VMEM((n_buf, ATTN_HEAD_DIM, ps), F32),
                            pltpu.SemaphoreType.DMA((2,))],
        ),
        out_shape=jax.ShapeDtypeStruct((db, 1, D_ATTN), F32),
        compiler_params=_cparams(("arbitrary",)),
        name="sample_attend",
    )(pt_flat, top_flat, q.reshape(db, 1, D_ATTN), k_new.reshape(db, 1, D_ATTN), v_new.reshape(db, 1, D_ATTN),
      kp5, vp5)
    return out.reshape(db, D_ATTN)


def _moba_sample(q, k_new, v_new, k_pool, v_pool, page_table, layer):
    db = q.shape[0]
    ps = k_pool.shape[2]
    n_pages = page_table.shape[1]
    ppb = MOBA_BLOCK // ps
    nbp = (n_pages * ps) // MOBA_BLOCK
    assert nbp >= 1 and (n_pages - nbp * ppb) * ps == 0, "the cached rows must fill whole MoBA blocks"
    n_sel = min(MOBA_TOPK, nbp)
    kp5 = jnp.transpose(k_pool, (0, 1, 3, 4, 2))
    vp5 = jnp.transpose(v_pool, (0, 1, 3, 4, 2))
    pt_flat = page_table.reshape(-1)
    ksum = _page_sums(kp5, pt_flat, layer, db, n_pages)
    top = _sample_select(q, ksum, nbp, ppb)
    top_flat = top[:, :, :n_sel].reshape(-1)
    return _sample_attend(q, k_new, v_new, kp5, vp5, pt_flat, top_flat, layer, n_pages, n_sel, ppb)


def _mlstm_step_body(qk_ref, vm_ref, if_ref, conv_ref, cw_ref, c_ref, n_ref, m_ref,
                     h_ref, c_out, n_out, m_out):
    tb = qk_ref.shape[0]
    w = cw_ref[...]
    lane = _iota((1, LANES), 1)

    def one(r):
        x = qk_ref[pl.ds(r, 1), :]
        cb = conv_ref[pl.ds(r, 1)][0]
        y = cb[0:1] * w[0:1]
        for j in range(1, CONV_W - 1):
            y = y + cb[j:j + 1] * w[j:j + 1]
        y = y + x * w[CONV_W - 1:CONV_W]
        qkc = _silu(y)
        gates = if_ref[pl.ds(r, 1), :]
        m_all = m_ref[pl.ds(r, 1), :]
        v_all = vm_ref[pl.ds(r, 1), :]
        for h in range(MLSTM_HEADS):
            sl = slice(h * MLSTM_HEAD_DIM, (h + 1) * MLSTM_HEAD_DIM)
            q = qkc[:, sl]
            k = qkc[:, D_MLSTM + h * MLSTM_HEAD_DIM:D_MLSTM + (h + 1) * MLSTM_HEAD_DIM] * (MLSTM_HEAD_DIM ** -0.5)
            v = v_all[:, sl]
            ig = gates[:, h:h + 1]
            fg = gates[:, MLSTM_HEADS + h:MLSTM_HEADS + h + 1]
            m_prev = m_all[:, h:h + 1]
            inter = _log_sigmoid(fg) + m_prev
            m_t = jnp.maximum(inter, ig)
            wgt = jnp.exp(ig - m_t)
            w0 = jnp.exp(inter - m_t)
            c_prev = c_ref[pl.ds(r, 1), h][0]
            n_prev = n_ref[pl.ds(r, 1), h]
            sqk = jnp.sum(q * k, axis=-1, keepdims=True) * wgt
            cq = _nt(jnp.broadcast_to(q, (SUBLANES, MLSTM_HEAD_DIM)), c_prev, precision=HI)[0:1]
            num = w0 * cq + sqk * v
            den = w0 * jnp.sum(n_prev * q, axis=-1, keepdims=True) + sqk
            h_ref[pl.ds(r, 1), sl] = num / jnp.maximum(jnp.abs(den), jnp.exp(-m_t))
            c_out[pl.ds(r, 1), h] = (w0 * c_prev + (wgt * _row_to_col(v)) * k)[None]
            n_out[pl.ds(r, 1), h] = w0 * n_prev + wgt * k
            m_all = jnp.where(lane == h, m_t, m_all)
        m_out[pl.ds(r, 1), :] = m_all

    for r in range(tb):
        one(r)


def _mlstm_step(qk_m, v_m, if_m, conv_state, conv_w, c, n, m_pad, tb):
    db = qk_m.shape[0]
    row = lambda w: pl.BlockSpec((tb, w), lambda i: (i, 0))
    c_spec = pl.BlockSpec((tb, MLSTM_HEADS, MLSTM_HEAD_DIM, MLSTM_HEAD_DIM), lambda i: (i, 0, 0, 0))
    n_spec = pl.BlockSpec((tb, MLSTM_HEADS, MLSTM_HEAD_DIM), lambda i: (i, 0, 0))
    return pl.pallas_call(
        _mlstm_step_body,
        grid=(db // tb,),
        in_specs=[row(2 * D_MLSTM), row(D_MLSTM), row(LANES),
                  pl.BlockSpec((tb, CONV_W - 1, 2 * D_MLSTM), lambda i: (i, 0, 0)),
                  pl.BlockSpec((CONV_W, 2 * D_MLSTM), lambda i: (0, 0)), c_spec, n_spec, row(LANES)],
        out_specs=[row(D_MLSTM), c_spec, n_spec, row(LANES)],
        out_shape=[jax.ShapeDtypeStruct((db, D_MLSTM), F32), jax.ShapeDtypeStruct(c.shape, F32),
                   jax.ShapeDtypeStruct(n.shape, F32), jax.ShapeDtypeStruct((db, LANES), F32)],
        compiler_params=_cparams(("arbitrary",)),
        name="mlstm_step",
    )(qk_m, v_m, if_m, conv_state, conv_w, c, n, m_pad)


def _sample_path(x, mod6, p, k_pool, v_pool, page_table, c0, n0, m0, conv0, layer):
    db, t, d = x.shape
    assert t == 1, "one new token per sequence"
    assert db % SUBLANES == 0
    xs = x.reshape(1, db, d)
    sh1, sc1, g1, sh2, sc2, g2 = [m.reshape(1, db, d) for m in mod6]
    q, k, v, _, _, _, qk_m, v_m, o_m, if_m, _ = _front(xs, sc1, sh1, p["norm1_g"], p["w_in"], p["b_if"], db)
    q, k, v, qk_m, v_m, if_m = [a[0] for a in (q, k, v, qk_m, v_m, if_m)]
    attn = _moba_sample(q, k, v, k_pool, v_pool, page_table, layer)
    m_pad = jnp.pad(m0, ((0, 0), (0, LANES - MLSTM_HEADS)))
    h_s, c_s, n_s, m_s = _mlstm_step(qk_m, v_m, if_m, conv0, p["conv_w"], c0, n0, m_pad, SUBLANES)
    y = _layer_tail(xs, (g1, sh2, sc2, g2), attn[None], h_s[None], o_m, p["attn_norm_g"], p["mlstm_norm_g"],
                    p["w_out"], p["norm2_g"], p["peer_w_query"], p["peer_sub_keys"], p["uv"], p["normf_g"],
                    db, min(64, db))
    conv_s = jnp.concatenate([conv0[:, 1:], qk_m[:, None, :]], axis=1)
    return y.reshape(db, t, d), k, v, c_s, n_s, m_s[:, :MLSTM_HEADS], conv_s


def kernel(x_prompt, x_sample, cache_k_pool, cache_v_pool, page_table, state_mlstm_C, state_mlstm_n, state_mlstm_m, state_conv, c_prompt, c_sample, w_ada, b_ada, norm1_g, w_in, b_if, conv_w, attn_norm_g, mlstm_norm_g, w_out, norm2_g, peer_w_query, peer_sub_keys, peer_u, peer_v, normf_g):
    b, s, d = x_prompt.shape
    db = x_sample.shape[0]
    l = 0
    p = dict(norm1_g=norm1_g[l], w_in=w_in[l], b_if=b_if[l], conv_w=conv_w[l], attn_norm_g=attn_norm_g[l],
             mlstm_norm_g=mlstm_norm_g[l], w_out=w_out[l], norm2_g=norm2_g[l], peer_w_query=peer_w_query[l],
             peer_sub_keys=peer_sub_keys[l], normf_g=normf_g,
             uv=jnp.concatenate([peer_u[l], peer_v[l]], axis=1)[:, None, :])
    rows = b + db
    pad = -rows % SUBLANES
    c_all = jnp.pad(jnp.concatenate([c_prompt, c_sample], axis=0), ((0, pad), (0, 0)))
    mod = _ada(c_all, w_ada[l], b_ada[l])
    mod_p = [mod[:b, i * d:(i + 1) * d] for i in range(6)]
    mod_s = [mod[b:rows, i * d:(i + 1) * d] for i in range(6)]
    y_p, k_p, v_p, c_p, n_p, m_p, conv_p = _prompt_path(x_prompt, mod_p, p)
    y_s, k_s, v_s, c_s, n_s, m_s, conv_s = _sample_path(
        x_sample, mod_s, p, cache_k_pool, cache_v_pool, page_table,
        state_mlstm_C[l], state_mlstm_n[l], state_mlstm_m[l], state_conv[l], l)
    hp = (b, s, ATTN_HEADS, ATTN_HEAD_DIM)
    hs = (db, 1, ATTN_HEADS, ATTN_HEAD_DIM)
    return (y_p, y_s, k_p.reshape(hp)[None], v_p.reshape(hp)[None], k_s.reshape(hs)[None], v_s.reshape(hs)[None],
            c_p[None], n_p[None], m_p[None], conv_p[None], c_s[None], n_s[None], m_s[None], conv_s[None])
```

```python
import functools

import jax
import jax.numpy as jnp
from jax import lax
from jax.experimental import pallas as pl
from jax.experimental.pallas import tpu as pltpu

F32 = jnp.float32
BF16 = jnp.bfloat16
I32 = jnp.int32
HI = lax.Precision.HIGHEST

SUBLANES = 8
LANES = 128
VMEM_LIMIT_BYTES = 56 * 1024 * 1024

D_MODEL = 1024
D_ATTN = 512
D_MLSTM = 512
ATTN_HEADS = 8
ATTN_HEAD_DIM = 64
MLSTM_HEADS = 4
MLSTM_HEAD_DIM = 128
MOBA_BLOCK = 256
MOBA_TOPK = 3
CONV_W = 4
PEER_HEADS = 8
PEER_N_KEYS = 128
PEER_TOPK = 16
PEER_HALF = 128
EPS = 1e-6
N_GATES = 2 * MLSTM_HEADS
C_Q, C_K, C_V, C_QK, C_VM, C_OM, C_IF = 0, 512, 1024, 1536, 2560, 3072, 3584
NEG = -1e30


def _cparams(sem):
    return pltpu.CompilerParams(dimension_semantics=sem, vmem_limit_bytes=VMEM_LIMIT_BYTES)


def _nt(a, b, **kw):
    return lax.dot_general(a, b, (((1,), (1,)), ((), ())), preferred_element_type=F32, **kw)


def _tn(a, b, **kw):
    return lax.dot_general(a, b, (((0,), (0,)), ((), ())), preferred_element_type=F32, **kw)


def _silu(x):
    return x * jax.nn.sigmoid(x)


def _log_sigmoid(x):
    return jnp.minimum(x, 0.0) - jnp.log1p(jnp.exp(-jnp.abs(x)))


def _iota(shape, dim):
    return lax.broadcasted_iota(I32, shape, dim)


def _ada_body(c_ref, w_ref, b_ref, o_ref):
    o_ref[...] = jnp.dot(_silu(c_ref[...]), w_ref[...], precision=HI, preferred_element_type=F32) + b_ref[...]


def _ada(c, w_ada, b_ada):
    rows, d = c.shape
    n = w_ada.shape[1]
    tn = 512
    return pl.pallas_call(
        _ada_body,
        grid=(n // tn,),
        in_specs=[pl.BlockSpec((rows, d), lambda j: (0, 0)),
                  pl.BlockSpec((d, tn), lambda j: (0, j)),
                  pl.BlockSpec((1, tn), lambda j: (0, j))],
        out_specs=pl.BlockSpec((rows, tn), lambda j: (0, j)),
        out_shape=jax.ShapeDtypeStruct((rows, n), F32),
        compiler_params=_cparams(("arbitrary",)),
        name="ada_mod",
    )(c, w_ada, b_ada.reshape(1, n))


def _front_body(x_ref, sc_ref, sh_ref, g_ref, w_ref, wif_ref, wift_ref, bif_ref, bift_ref,
                q_ref, k_ref, v_ref, qb_ref, kb_ref, vb_ref, qk_ref, vm_ref, om_ref, if_ref, ift_ref):
    x = x_ref[0]
    h = x * lax.rsqrt(jnp.mean(x * x, axis=-1, keepdims=True) + EPS) * g_ref[...]
    h = h * (1.0 + sc_ref[0]) + sh_ref[0]
    hb = h.astype(BF16)

    def seg(lo, hi):
        return jnp.dot(hb, w_ref[:, lo:hi], preferred_element_type=F32)

    q = seg(C_Q, C_K)
    q_ref[0] = q
    qb_ref[0] = (q * (ATTN_HEAD_DIM ** -0.5)).astype(BF16)
    k = seg(C_K, C_V)
    k_ref[0] = k
    kb_ref[0] = k.astype(BF16)
    v = seg(C_V, C_QK)
    v_ref[0] = v
    vb_ref[0] = v.astype(BF16)
    qk_ref[0] = seg(C_QK, C_VM)
    vm_ref[0] = seg(C_VM, C_OM)
    om_ref[0] = seg(C_OM, C_IF)
    if_ref[0] = jnp.dot(h, wif_ref[...], precision=HI, preferred_element_type=F32) + bif_ref[...]
    ift_ref[0] = _nt(wift_ref[...], h, precision=HI) + bift_ref[...]


def _front(x, sc, sh, norm_g, w_in, b_if, tm):
    b, s, d = x.shape
    per_row = sc.shape[1] != 1
    mod_spec = (pl.BlockSpec((1, tm, d), lambda bi, i: (bi, i, 0)) if per_row
                else pl.BlockSpec((1, 1, d), lambda bi, i: (bi, 0, 0)))
    w_main = w_in[:, :C_IF].astype(BF16)
    w_if = jnp.pad(w_in[:, C_IF:], ((0, 0), (0, LANES - N_GATES)))
    w_ift = w_in[:, C_IF:].T
    b_row = jnp.pad(b_if, (0, LANES - N_GATES)).reshape(1, LANES)
    b_col = b_if.reshape(N_GATES, 1)
    const = lambda shape: pl.BlockSpec(shape, lambda bi, i: (0,) * len(shape))
    tok = lambda w: pl.BlockSpec((1, tm, w), lambda bi, i: (bi, i, 0))
    out_shapes = [
        jax.ShapeDtypeStruct((b, s, D_ATTN), F32),
        jax.ShapeDtypeStruct((b, s, D_ATTN), F32),
        jax.ShapeDtypeStruct((b, s, D_ATTN), F32),
        jax.ShapeDtypeStruct((b, s, D_ATTN), BF16),
        jax.ShapeDtypeStruct((b, s, D_ATTN), BF16),
        jax.ShapeDtypeStruct((b, s, D_ATTN), BF16),
        jax.ShapeDtypeStruct((b, s, 2 * D_MLSTM), F32),
        jax.ShapeDtypeStruct((b, s, D_MLSTM), F32),
        jax.ShapeDtypeStruct((b, s, D_MLSTM), F32),
        jax.ShapeDtypeStruct((b, s, LANES), F32),
        jax.ShapeDtypeStruct((b, N_GATES, s), F32),
    ]
    out_specs = [tok(D_ATTN)] * 6 + [tok(2 * D_MLSTM), tok(D_MLSTM), tok(D_MLSTM), tok(LANES),
                                     pl.BlockSpec((1, N_GATES, tm), lambda bi, i: (bi, 0, i))]
    return pl.pallas_call(
        _front_body,
        grid=(b, s // tm),
        in_specs=[tok(d), mod_spec, mod_spec, const((1, d)), const((d, C_IF)), const((d, LANES)),
                  const((N_GATES, d)), const((1, LANES)), const((N_GATES, 1))],
        out_specs=out_specs,
        out_shape=out_shapes,
        compiler_params=_cparams(("arbitrary", "arbitrary")),
        name="front",
    )(x, sc, sh, norm_g.reshape(1, d), w_main, w_if, w_ift, b_row, b_col)


def _km_body(k_ref, km_ref, *, hb):
    n = pl.program_id(1)

    @pl.when(n == 0)
    def _():
        km_ref[...] = jnp.zeros_like(km_ref)

    lw = km_ref.shape[2]
    ones = jnp.full((MOBA_BLOCK, lw), 1.0 / MOBA_BLOCK, F32)
    mean_b = _tn(k_ref[0], ones, precision=HI)
    row = _iota((D_ATTN, lw), 0)
    col = _iota((D_ATTN, lw), 1)
    km_ref[0] += jnp.where(col == (row // ATTN_HEAD_DIM) * hb + n, mean_b, 0.0)


def _block_means(k, nb, hb):
    lw = ATTN_HEADS * hb
    b = k.shape[0]
    return pl.pallas_call(
        functools.partial(_km_body, hb=hb),
        grid=(b, nb),
        in_specs=[pl.BlockSpec((1, MOBA_BLOCK, D_ATTN), lambda bi, n: (bi, n, 0))],
        out_specs=pl.BlockSpec((1, D_ATTN, lw), lambda bi, n: (bi, 0, 0)),
        out_shape=jax.ShapeDtypeStruct((b, D_ATTN, lw), F32),
        compiler_params=_cparams(("arbitrary", "arbitrary")),
        name="moba_block_means",
    )(k)


def _sel_body(q_ref, km_ref, o_ref, *, hb, n_top):
    own = pl.program_id(1)
    s = jnp.dot(q_ref[0], km_ref[0], precision=HI, preferred_element_type=F32)
    shape = s.shape
    lw = shape[1]
    lane = _iota(shape, 1)
    blk = lane & (hb - 1)
    head = lane >> (hb.bit_length() - 1)
    valid = (blk < own) & (head < ATTN_HEADS)
    sel = jnp.zeros(shape, jnp.bool_)
    for h in range(ATTN_HEADS):
        sm = jnp.where(valid & (head == h), s, -jnp.inf)
        for _ in range(n_top):
            m = jnp.max(sm, axis=-1, keepdims=True)
            i = jnp.min(jnp.where(sm == m, lane, lw), axis=-1, keepdims=True)
            hit = lane == i
            sel = sel | (hit & (m > -jnp.inf))
            sm = jnp.where(hit, -jnp.inf, sm)
    o_ref[0] = jnp.where(sel, 0.0, NEG)


def _moba_select(q, km, nb, hb):
    b, s, _ = q.shape
    lw = km.shape[2]
    return pl.pallas_call(
        functools.partial(_sel_body, hb=hb, n_top=min(MOBA_TOPK, nb)),
        grid=(b, s // MOBA_BLOCK),
        in_specs=[pl.BlockSpec((1, MOBA_BLOCK, D_ATTN), lambda bi, i: (bi, i, 0)),
                  pl.BlockSpec((1, D_ATTN, lw), lambda bi, i: (bi, 0, 0))],
        out_specs=pl.BlockSpec((1, MOBA_BLOCK, lw), lambda bi, i: (bi, i, 0)),
        out_shape=jax.ShapeDtypeStruct((b, s, lw), F32),
        compiler_params=_cparams(("arbitrary", "arbitrary")),
        name="moba_select",
    )(q, km)


def _attn_body(q_ref, k_ref, v_ref, sb_ref, lid_ref, o_ref, *, hb):
    own = pl.program_id(2)
    tq = MOBA_BLOCK
    tiles = q_ref.shape[2] // LANES
    n_heads = 2 * tiles
    lane = _iota((tq, LANES), 1)
    low = lane < ATTN_HEAD_DIM
    sb = sb_ref[0]
    one = jnp.ones((tq, LANES), BF16)
    zero = jnp.zeros((tq, LANES), BF16)
    lane_b = lid_ref[...]
    low_b = lane_b < ATTN_HEAD_DIM
    qs, mine, bias0 = [], [], []
    for hh in range(n_heads):
        t = hh // 2
        q = q_ref[0, :, t * LANES:(t + 1) * LANES]
        is_low = hh % 2 == 0
        first = ATTN_HEAD_DIM if is_low else 0
        tile, off = divmod(hh * hb, LANES)
        bias = sb[:, tile * LANES:(tile + 1) * LANES]
        shift = (first - off) % LANES
        if shift:
            bias = pltpu.roll(bias, shift, 1)
        keep = low_b if is_low else jnp.logical_not(low_b)
        qs.append(jnp.where(keep, q, bias.astype(BF16)))
        mine.append(keep)
        bias0.append(first)
    row = _iota((tq, tq), 0)
    col = _iota((tq, tq), 1)

    def attend(j, state, is_own):
        start = pl.multiple_of(j * tq, tq)
        new = []
        for hh in range(n_heads):
            sl = slice((hh // 2) * LANES, (hh // 2 + 1) * LANES)
            kj = k_ref[0, pl.ds(start, tq), sl]
            vj = v_ref[0, pl.ds(start, tq), sl]
            m, acc = state[hh]
            if is_own:
                s = _nt(qs[hh], jnp.where(mine[hh], kj, zero))
                s = jnp.where(col <= row, s, -jnp.inf)
            else:
                hot = lane_b == (bias0[hh] + j).astype(F32).astype(BF16)
                s = _nt(qs[hh], jnp.where(mine[hh], kj, jnp.where(hot, one, zero)))
            m_new = jnp.maximum(m, jnp.max(s, axis=-1, keepdims=True))
            p = jnp.exp(s - m_new)
            acc = jnp.exp(m - m_new) * acc + jnp.dot(p.astype(BF16), jnp.where(mine[hh], vj, one),
                                                      preferred_element_type=F32)
            new.append((m_new, acc))
        return tuple(new)

    init = tuple((jnp.full((tq, 1), -jnp.inf, F32), jnp.zeros((tq, LANES), F32)) for _ in range(n_heads))
    state = attend(own, init, True)
    state = lax.fori_loop(0, own, lambda j, st: attend(j, st, False), state)
    for t in range(tiles):
        outs = []
        for hh in (2 * t, 2 * t + 1):
            acc = state[hh][1]
            outs.append(acc / pltpu.roll(acc, ATTN_HEAD_DIM, 1))
        o_ref[0, :, t * LANES:(t + 1) * LANES] = jnp.where(low, outs[0], outs[1])


ATTN_LANES_PER_STEP = 2 * LANES


def _moba_attend(qb, kb, vb, selbias, hb):
    b, s, _ = qb.shape
    gw = ATTN_LANES_PER_STEP
    groups = D_ATTN // gw
    sw = selbias.shape[2] // groups
    assert sw % LANES == 0
    return pl.pallas_call(
        functools.partial(_attn_body, hb=hb),
        grid=(b, groups, s // MOBA_BLOCK),
        in_specs=[pl.BlockSpec((1, MOBA_BLOCK, gw), lambda bi, p, i: (bi, i, p)),
                  pl.BlockSpec((1, s, gw), lambda bi, p, i: (bi, 0, p)),
                  pl.BlockSpec((1, s, gw), lambda bi, p, i: (bi, 0, p)),
                  pl.BlockSpec((1, MOBA_BLOCK, sw), lambda bi, p, i: (bi, i, p)),
                  pl.BlockSpec((MOBA_BLOCK, LANES), lambda bi, p, i: (0, 0))],
        out_specs=pl.BlockSpec((1, MOBA_BLOCK, gw), lambda bi, p, i: (bi, i, p)),
        out_shape=jax.ShapeDtypeStruct((b, s, D_ATTN), F32),
        compiler_params=_cparams(("arbitrary", "arbitrary", "arbitrary")),
        name="moba_attend",
    )(qb, kb, vb, selbias, jnp.broadcast_to(jnp.arange(LANES, dtype=F32).astype(BF16), (MOBA_BLOCK, LANES)))


MIN_HEAD_STRIDE = 32


def _moba_prompt(q, k, qb, kb, vb):
    s = q.shape[1]
    assert s % MOBA_BLOCK == 0
    nb = s // MOBA_BLOCK
    assert nb & (nb - 1) == 0, "block count must be a power of two"
    hb = max(nb, MIN_HEAD_STRIDE)
    assert hb <= ATTN_HEAD_DIM, "the per-block bias must fit in the free half of a lane tile"
    km = _block_means(k, nb, hb)
    selbias = _moba_select(q, km, nb, hb)
    return _moba_attend(qb, kb, vb, selbias, hb)


def _mlstm_body(qk_ref, vm_ref, if_ref, ift_ref, cw_ref, h_ref, c_ref, n_ref, m_ref, prev_scr):
    i = pl.program_id(1)
    L = qk_ref.shape[1]

    @pl.when(i == 0)
    def _():
        prev_scr[...] = jnp.zeros_like(prev_scr)
        c_ref[...] = jnp.zeros_like(c_ref)
        n_ref[...] = jnp.zeros_like(n_ref)
        m_ref[...] = jnp.zeros_like(m_ref)

    x = qk_ref[0]
    prev = prev_scr[...]
    w = cw_ref[...]
    rows = _iota(x.shape, 0)
    y = x * w[CONV_W - 1:CONV_W]
    for sft in range(1, CONV_W):
        xs = jnp.where(rows >= sft, pltpu.roll(x, sft, 0), pltpu.roll(prev, sft, 0))
        y = y + xs * w[CONV_W - 1 - sft:CONV_W - sft]
    prev_scr[...] = x
    qkc = _silu(y)

    gcol = if_ref[0]
    grow = ift_ref[0]
    tri_r = _iota((L, L), 0)
    tri_c = _iota((L, L), 1)
    lower = tri_c <= tri_r
    b_col = jnp.dot(lower.astype(F32), _log_sigmoid(gcol), precision=HI, preferred_element_type=F32)
    b_row = jnp.dot(_log_sigmoid(grow), (tri_r <= tri_c).astype(F32), precision=HI, preferred_element_type=F32)
    m_all = m_ref[0]
    lane = _iota(m_all.shape, 1)
    for h in range(MLSTM_HEADS):
        sl = slice(h * MLSTM_HEAD_DIM, (h + 1) * MLSTM_HEAD_DIM)
        qh = qkc[:, sl]
        kh = qkc[:, D_MLSTM + h * MLSTM_HEAD_DIM:D_MLSTM + (h + 1) * MLSTM_HEAD_DIM] * (MLSTM_HEAD_DIM ** -0.5)
        vh = vm_ref[0][:, sl]
        fh = MLSTM_HEADS + h
        bc = b_col[:, fh:fh + 1]
        br = b_row[fh:fh + 1, :]
        ic = gcol[:, h:h + 1]
        ir = grow[h:h + 1, :]
        m_prev = m_all[:, h:h + 1]
        c_prev = c_ref[0, h]
        n_prev = n_ref[0, h:h + 1, :]
        dmat = jnp.where(lower, bc - br + ir, -jnp.inf)
        inter = bc + m_prev
        m_t = jnp.maximum(inter, jnp.max(dmat, axis=-1, keepdims=True))
        wmat = jnp.exp(dmat - m_t)
        w0 = jnp.exp(inter - m_t)
        qb, kb, vb = qh.astype(BF16), kh.astype(BF16), vh.astype(BF16)
        sqk = _nt(qb, kb) * wmat
        num = w0 * _nt(qb, c_prev.astype(BF16)) + jnp.dot(sqk.astype(BF16), vb, preferred_element_type=F32)
        den = w0 * jnp.sum(qh * n_prev, axis=-1, keepdims=True) + jnp.sum(sqk, axis=-1, keepdims=True)
        h_ref[0, :, sl] = num / jnp.maximum(jnp.abs(den), jnp.exp(-m_t))
        m_new = m_t[L - 1:L, :]
        b_last = bc[L - 1:L, :]
        g = jnp.exp(b_last - bc + ic - m_new)
        decay = jnp.exp(b_last + m_prev - m_new)
        c_ref[0, h] = decay * c_prev + _tn((g * vh).astype(BF16), kb)
        n_ref[0, h:h + 1, :] = decay * n_prev + jnp.sum(g * kh, axis=0, keepdims=True)
        m_all = jnp.where(lane == h, m_new, m_all)
    m_ref[0] = m_all


def _mlstm_prompt(qk_m, v_m, if_m, if_t, conv_w, chunk):
    b, s, _ = qk_m.shape
    tok = lambda w: pl.BlockSpec((1, chunk, w), lambda bi, i: (bi, i, 0))
    return pl.pallas_call(
        _mlstm_body,
        grid=(b, s // chunk),
        in_specs=[tok(2 * D_MLSTM), tok(D_MLSTM), tok(LANES),
                  pl.BlockSpec((1, N_GATES, chunk), lambda bi, i: (bi, 0, i)),
                  pl.BlockSpec((CONV_W, 2 * D_MLSTM), lambda bi, i: (0, 0))],
        out_specs=[tok(D_MLSTM),
                   pl.BlockSpec((1, MLSTM_HEADS, MLSTM_HEAD_DIM, MLSTM_HEAD_DIM), lambda bi, i: (bi, 0, 0, 0)),
                   pl.BlockSpec((1, MLSTM_HEADS, MLSTM_HEAD_DIM), lambda bi, i: (bi, 0, 0)),
                   pl.BlockSpec((1, 1, LANES), lambda bi, i: (bi, 0, 0))],
        out_shape=[jax.ShapeDtypeStruct((b, s, D_MLSTM), F32),
                   jax.ShapeDtypeStruct((b, MLSTM_HEADS, MLSTM_HEAD_DIM, MLSTM_HEAD_DIM), F32),
                   jax.ShapeDtypeStruct((b, MLSTM_HEADS, MLSTM_HEAD_DIM), F32),
                   jax.ShapeDtypeStruct((b, 1, LANES), F32)],
        scratch_shapes=[pltpu.VMEM((chunk, 2 * D_MLSTM), F32)],
        compiler_params=_cparams(("arbitrary", "arbitrary")),
        name="mlstm_prompt",
    )(qk_m, v_m, if_m, if_t, conv_w)


def _back_body(x_ref, a_ref, hm_ref, om_ref, g1_ref, sc_ref, sh_ref, ag_ref, mg_ref, wo_ref, n2_ref, wq_ref,
               x1_ref, h2_ref, pq_ref):
    a = a_ref[0]
    tm = a.shape[0]
    lane = _iota((tm, LANES), 1)
    low = lane < ATTN_HEAD_DIM
    proj = jnp.zeros((tm, D_MODEL), F32)
    for t in range(D_ATTN // LANES):
        sl = slice(t * LANES, (t + 1) * LANES)
        at = a[:, sl]
        sq = at * at
        s_lo = jnp.sum(jnp.where(low, sq, 0.0), axis=-1, keepdims=True)
        s_hi = jnp.sum(jnp.where(low, 0.0, sq), axis=-1, keepdims=True)
        scale = jnp.where(low, lax.rsqrt(s_lo / ATTN_HEAD_DIM + EPS), lax.rsqrt(s_hi / ATTN_HEAD_DIM + EPS))
        an = at * scale * ag_ref[:, sl]
        proj = proj + jnp.dot(an.astype(BF16), wo_ref[sl, :], preferred_element_type=F32)
    hm = hm_ref[0]
    om = om_ref[0]
    for t in range(MLSTM_HEADS):
        sl = slice(t * LANES, (t + 1) * LANES)
        ht = hm[:, sl]
        hn = ht * lax.rsqrt(jnp.mean(ht * ht, axis=-1, keepdims=True) + EPS) * mg_ref[:, sl]
        mm = jax.nn.sigmoid(om[:, sl]) * hn
        proj = proj + jnp.dot(mm.astype(BF16), wo_ref[D_ATTN + t * LANES:D_ATTN + (t + 1) * LANES, :],
                              preferred_element_type=F32)
    x1 = x_ref[0] + g1_ref[0] * proj
    x1_ref[0] = x1
    h2 = x1 * lax.rsqrt(jnp.mean(x1 * x1, axis=-1, keepdims=True) + EPS) * n2_ref[...]
    h2 = h2 * (1.0 + sc_ref[0]) + sh_ref[0]
    h2_ref[0] = h2
    pq_ref[0] = jnp.dot(h2.astype(BF16), wq_ref[...], preferred_element_type=F32)


def _back(x, attn, h_m, o_m, g1, sc2, sh2, attn_g, mlstm_g, w_out, norm2_g, w_query, tm):
    b, s, d = x.shape
    per_row = g1.shape[1] != 1
    mod_spec = (pl.BlockSpec((1, tm, d), lambda bi, i: (bi, i, 0)) if per_row
                else pl.BlockSpec((1, 1, d), lambda bi, i: (bi, 0, 0)))
    nq = w_query.shape[1]
    const = lambda shape: pl.BlockSpec(shape, lambda bi, i: (0,) * len(shape))
    tok = lambda w: pl.BlockSpec((1, tm, w), lambda bi, i: (bi, i, 0))
    return pl.pallas_call(
        _back_body,
        grid=(b, s // tm),
        in_specs=[tok(d), tok(D_ATTN), tok(D_MLSTM), tok(D_MLSTM), mod_spec, mod_spec, mod_spec,
                  const((1, D_ATTN)), const((1, D_MLSTM)), const((d, d)), const((1, d)), const((d, nq))],
        out_specs=[tok(d), tok(d), tok(nq)],
        out_shape=[jax.ShapeDtypeStruct((b, s, d), F32), jax.ShapeDtypeStruct((b, s, d), F32),
                   jax.ShapeDtypeStruct((b, s, nq), F32)],
        compiler_params=_cparams(("arbitrary", "arbitrary")),
        name="back",
    )(x, attn, h_m, o_m, g1, sc2, sh2, attn_g.reshape(1, D_ATTN), mlstm_g.reshape(1, D_MLSTM),
      w_out.astype(BF16), norm2_g.reshape(1, d), w_query.astype(BF16))


def _peer_route_body(pq_ref, sk_ref, idx_ref, gate_ref):
    tp = pq_ref.shape[0]
    k = PEER_TOPK
    n_exp = PEER_N_KEYS * PEER_N_KEYS
    out_row = _iota((k, tp), 0)

    def extract(score, tag, fill):
        vals = jnp.zeros((k, tp), F32)
        tags = jnp.zeros((k, tp), F32)
        for r in range(k):
            m = jnp.max(score, axis=0, keepdims=True)
            t = jnp.min(jnp.where(score == m, tag, fill), axis=0, keepdims=True)
            vals = jnp.where(out_row == r, m, vals)
            tags = jnp.where(out_row == r, t, tags)
            score = jnp.where(tag == t, -jnp.inf, score)
        return vals, tags

    key_row = _iota((PEER_N_KEYS, tp), 0).astype(F32)
    pos = _iota((k * k, tp), 0).astype(F32)
    gates, ids = [], []
    for h in range(PEER_HEADS):
        halves = []
        for p in range(2):
            c0 = (2 * h + p) * PEER_HALF
            qh = pq_ref[:, c0:c0 + PEER_HALF].astype(BF16)
            s = _nt(sk_ref[2 * h + p].astype(BF16), qh)
            halves.append(extract(s, key_row, float(PEER_N_KEYS)))
        (v0, i0), (v1, i1) = halves
        cand = jnp.concatenate([v0[a:a + 1] + v1 for a in range(k)], axis=0)
        eid = jnp.concatenate([i0[a:a + 1] * PEER_N_KEYS + i1 for a in range(k)], axis=0)
        comb = pos * n_exp + eid
        c_s, c_key = extract(cand, comb, float(k * k * n_exp))
        e = jnp.exp(c_s - c_s[0:1])
        gates.append(e / jnp.sum(e, axis=0, keepdims=True))
        ids.append(c_key)
    gate_ref[...] = jnp.concatenate(gates, axis=0).T
    keys = jnp.concatenate(ids, axis=0).T.astype(I32)
    idx_ref[...] = keys & (n_exp - 1)


def _peer_route(pq, sub_keys):
    n = pq.shape[0]
    tp = LANES
    n_pad = -(-n // tp) * tp
    pq = jnp.pad(pq, ((0, n_pad - n), (0, 0)))
    sk = sub_keys.reshape(PEER_HEADS * 2, PEER_N_KEYS, PEER_HALF)
    idx, gate = pl.pallas_call(
        _peer_route_body,
        grid=(n_pad // tp,),
        in_specs=[pl.BlockSpec((tp, pq.shape[1]), lambda i: (i, 0)),
                  pl.BlockSpec(sk.shape, lambda i: (0, 0, 0))],
        out_specs=[pl.BlockSpec((tp, LANES), lambda i: (i, 0)), pl.BlockSpec((tp, LANES), lambda i: (i, 0))],
        out_shape=[jax.ShapeDtypeStruct((n_pad, LANES), I32), jax.ShapeDtypeStruct((n_pad, LANES), F32)],
        compiler_params=_cparams(("arbitrary",)),
        name="peer_route",
    )(pq, sk)
    return idx[:n], gate[:n]


N_SEL = PEER_HEADS * PEER_TOPK


def _pack_bf16_pairs(a):
    half = a.shape[1] // 2
    b = lax.bitcast_convert_type(a.astype(BF16), jnp.uint16).astype(jnp.uint32)
    return (b[:, :half] << 16) | b[:, half:]


def _peer_expert_body(idx_ref, gate_ref, h2_ref, x1_ref, g2_ref, nf_ref, uv_ref, y_ref, buf, sem, peer_scr):
    tt = h2_ref.shape[1]
    row_w = uv_ref.shape[2]
    half = D_MODEL // 2

    def issue(t, slot):
        for k in range(N_SEL):
            e = idx_ref[0, 0, t * N_SEL + k]
            pltpu.make_async_copy(uv_ref.at[e], buf.at[slot, pl.ds(k, 1)],
                                  sem.at[slot]).start(priority=k % 2)

    def wait_all(slot):
        pltpu.make_async_copy(uv_ref.at[pl.ds(0, N_SEL), 0], buf.at[slot], sem.at[slot]).wait()

    rr = _iota((N_SEL, LANES), 0)
    cc = _iota((N_SEL, LANES), 1)
    diag = rr == cc

    def unpack(words):
        hi = lax.bitcast_convert_type(words & jnp.uint32(0xFFFF0000), F32)
        lo = lax.bitcast_convert_type(words << 16, F32)
        return hi, lo

    def compute(t, slot):
        x = h2_ref[0, pl.ds(t, 1), :]
        u_hi, u_lo = unpack(buf[slot, :, 0:half])
        s = jnp.sum(u_hi * x[:, 0:half] + u_lo * x[:, half:D_MODEL], axis=-1, keepdims=True)
        act = 0.5 * s * (1.0 + lax.erf(s * (0.5 ** 0.5)))
        g_row = gate_ref[0, pl.ds(t, 1), :]
        g_col = jnp.sum(jnp.where(diag, jnp.broadcast_to(g_row, (N_SEL, LANES)), 0.0), axis=-1, keepdims=True)
        w = g_col * act
        v_hi, v_lo = unpack(buf[slot, :, half:2 * half])
        peer_scr[pl.ds(t, 1), 0:half] = jnp.sum(v_hi * w, axis=0, keepdims=True)
        peer_scr[pl.ds(t, 1), half:D_MODEL] = jnp.sum(v_lo * w, axis=0, keepdims=True)

    issue(0, 0)

    def token_pair(i, carry):
        t0 = 2 * i
        issue(t0 + 1, 1)
        wait_all(0)
        compute(t0, 0)

        @pl.when(t0 + 2 < tt)
        def _():
            issue(t0 + 2, 0)

        wait_all(1)
        compute(t0 + 1, 1)
        return carry

    lax.fori_loop(0, tt // 2, token_pair, 0)
    y = x1_ref[0] + g2_ref[0] * peer_scr[...]
    y_ref[0] = y * lax.rsqrt(jnp.mean(y * y, axis=-1, keepdims=True) + EPS) * nf_ref[...]


def _peer_experts(idx, gate, h2, x1, g2, normf_g, uv, tt):
    b, s, d = h2.shape
    assert tt % 2 == 0 and s % tt == 0
    nt = s // tt
    per_row = g2.shape[1] != 1
    mod_spec = (pl.BlockSpec((1, tt, d), lambda bi, i: (bi, i, 0)) if per_row
                else pl.BlockSpec((1, 1, d), lambda bi, i: (bi, 0, 0)))
    tok = lambda w: pl.BlockSpec((1, tt, w), lambda bi, i: (bi, i, 0))
    idx3 = idx.reshape(b * nt, 1, tt * N_SEL)
    return pl.pallas_call(
        _peer_expert_body,
        grid=(b, nt),
        in_specs=[pl.BlockSpec((1, 1, tt * N_SEL), lambda bi, i: (bi * nt + i, 0, 0), memory_space=pltpu.SMEM),
                  tok(LANES), tok(d), tok(d), mod_spec,
                  pl.BlockSpec((1, d), lambda bi, i: (0, 0)),
                  pl.BlockSpec(memory_space=pl.ANY)],
        out_specs=tok(d),
        out_shape=jax.ShapeDtypeStruct((b, s, d), F32),
        scratch_shapes=[pltpu.VMEM((2, N_SEL, uv.shape[2]), jnp.uint32), pltpu.SemaphoreType.DMA((2,)),
                        pltpu.VMEM((tt, d), F32)],
        compiler_params=_cparams(("arbitrary", "arbitrary")),
        name="peer_experts",
    )(idx3, gate.reshape(b, s, LANES), h2, x1, g2, normf_g.reshape(1, d), uv)


def _layer_tail(x, mods, attn, h_m, o_m, attn_g, mlstm_g, w_out, norm2_g, w_query, sub_keys, uv, normf_g, tm, tt):
    g1, sh2, sc2, g2 = mods
    b, s, d = x.shape
    x1, h2, pq = _back(x, attn, h_m, o_m, g1, sc2, sh2, attn_g, mlstm_g, w_out, norm2_g, w_query, tm)
    idx, gate = _peer_route(pq.reshape(b * s, -1), sub_keys)
    return _peer_experts(idx, gate, h2, x1, g2, normf_g, uv, tt)


def _prompt_path(x, mod6, p):
    b, s, d = x.shape
    sh1, sc1, g1, sh2, sc2, g2 = [m.reshape(b, 1, d) for m in mod6]
    tm = min(512, s)
    q, k, v, qb, kb, vb, qk_m, v_m, o_m, if_m, if_t = _front(x, sc1, sh1, p["norm1_g"], p["w_in"], p["b_if"], tm)
    attn = _moba_prompt(q, k, qb, kb, vb)
    h_m, c_p, n_p, m_p = _mlstm_prompt(qk_m, v_m, if_m, if_t, p["conv_w"], min(256, s))
    y = _layer_tail(x, (g1, sh2, sc2, g2), attn, h_m, o_m, p["attn_norm_g"], p["mlstm_norm_g"], p["w_out"],
                    p["norm2_g"], p["peer_w_query"], p["peer_sub_keys"], p["uv"], p["normf_g"],
                    min(256, s), min(64, s))
    conv_p = qk_m[:, s - (CONV_W - 1):, :]
    return y, k, v, c_p, n_p, m_p[:, 0, :MLSTM_HEADS], conv_p


PAGES_PER_STEP = 8


def _page_sum_body(pt_ref, *refs, n_in):
    del pt_ref
    g = pl.program_id(1)
    out_ref = refs[n_in]

    @pl.when(g == 0)
    def _():
        out_ref[...] = jnp.zeros_like(out_ref)

    lane = _iota((D_ATTN, LANES), 1)
    acc = out_ref[0]
    for j in range(n_in):
        page = refs[j][0, 0].reshape(D_ATTN, refs[j].shape[-1])
        acc = jnp.where(lane == g * n_in + j, jnp.sum(page, axis=-1, keepdims=True), acc)
    out_ref[0] = acc


def _page_sums(kp5, pt_flat, layer, db, n_pages):
    assert n_pages <= LANES
    n_in = min(PAGES_PER_STEP, n_pages)
    assert n_pages % n_in == 0
    ps = kp5.shape[-1]

    def page_spec(j):
        return pl.BlockSpec((1, 1, ATTN_HEADS, ATTN_HEAD_DIM, ps),
                            lambda d, g, pt: (layer, pt[d * n_pages + g * n_in + j], 0, 0, 0))

    return pl.pallas_call(
        functools.partial(_page_sum_body, n_in=n_in),
        grid_spec=pltpu.PrefetchScalarGridSpec(
            num_scalar_prefetch=1,
            grid=(db, n_pages // n_in),
            in_specs=[page_spec(j) for j in range(n_in)],
            out_specs=pl.BlockSpec((1, D_ATTN, LANES), lambda d, g, pt: (d, 0, 0)),
        ),
        out_shape=jax.ShapeDtypeStruct((db, D_ATTN, LANES), F32),
        compiler_params=_cparams(("arbitrary", "arbitrary")),
        name="sample_page_sums",
    )(pt_flat, *([kp5] * n_in))


def _sample_select_body(q_ref, ks_ref, o_ref, *, nbp, ppb):
    q = jnp.broadcast_to(q_ref[0], (ATTN_HEADS, D_ATTN))
    qbd = jnp.where(_iota(q.shape, 1) // ATTN_HEAD_DIM == _iota(q.shape, 0), q, 0.0)
    s_page = jnp.dot(qbd, ks_ref[0], precision=HI, preferred_element_type=F32)
    pr = _iota((LANES, LANES), 0)
    pc = _iota((LANES, LANES), 1)
    merge = jnp.where((pr // ppb == pc) & (pc < nbp), 1.0 / MOBA_BLOCK, 0.0)
    s = jnp.dot(s_page, merge, precision=HI, preferred_element_type=F32)
    lane = _iota(s.shape, 1)
    s = jnp.where(lane < nbp, s, -jnp.inf)
    top = jnp.zeros(s.shape, I32)
    for r in range(min(MOBA_TOPK, nbp)):
        m = jnp.max(s, axis=-1, keepdims=True)
        i = jnp.min(jnp.where(s == m, lane, LANES), axis=-1, keepdims=True)
        top = jnp.where(lane == r, i, top)
        s = jnp.where(lane == i, -jnp.inf, s)
    o_ref[0] = top


def _sample_select(q, ksum, nbp, ppb):
    db = q.shape[0]
    return pl.pallas_call(
        functools.partial(_sample_select_body, nbp=nbp, ppb=ppb),
        grid=(db,),
        in_specs=[pl.BlockSpec((1, 1, D_ATTN), lambda d: (d, 0, 0)),
                  pl.BlockSpec((1, D_ATTN, LANES), lambda d: (d, 0, 0))],
        out_specs=pl.BlockSpec((1, ATTN_HEADS, LANES), lambda d: (d, 0, 0)),
        out_shape=jax.ShapeDtypeStruct((db, ATTN_HEADS, LANES), I32),
        compiler_params=_cparams(("arbitrary",)),
        name="sample_select",
    )(q.reshape(db, 1, D_ATTN), ksum)


def _row_to_col(row):
    eye = _iota((LANES, LANES), 0) == _iota((LANES, LANES), 1)
    return jnp.sum(jnp.where(eye, jnp.broadcast_to(row, (LANES, LANES)), 0.0), axis=-1, keepdims=True)


def _col_to_row(col):
    eye = _iota((LANES, LANES), 0) == _iota((LANES, LANES), 1)
    return jnp.sum(jnp.where(eye, jnp.broadcast_to(col, (LANES, LANES)), 0.0), axis=0, keepdims=True)


def _sample_attend_body(pt_ref, top_ref, q_ref, kn_ref, vn_ref, kp_ref, vp_ref, o_ref, kbuf, vbuf, sem,
                        *, layer, n_pages, n_sel, ppb):
    d = pl.program_id(0)
    per_head = n_sel * ppb

    def copies(h, r, j):
        blk = top_ref[(d * ATTN_HEADS + h) * n_sel + r]
        phys = pt_ref[d * n_pages + blk * ppb + j]
        slot = h * per_head + r * ppb + j
        return (pltpu.make_async_copy(kp_ref.at[layer, phys, h], kbuf.at[slot], sem.at[0]),
                pltpu.make_async_copy(vp_ref.at[layer, phys, h], vbuf.at[slot], sem.at[1]))

    todo = [(h, r, j) for h in range(ATTN_HEADS) for r in range(n_sel) for j in range(ppb)]
    for hrj in todo:
        for c in copies(*hrj):
            c.start()
    q_row = q_ref[0] * (ATTN_HEAD_DIM ** -0.5)
    k_row = kn_ref[0]
    v_row = vn_ref[0]
    tiles = D_ATTN // LANES
    q_col = jnp.concatenate([_row_to_col(q_row[:, t * LANES:(t + 1) * LANES]) for t in range(tiles)], axis=0)
    head_of_lane = _iota((1, D_ATTN), 1) // ATTN_HEAD_DIM
    qk_own = q_row * k_row
    for hrj in todo:
        for c in copies(*hrj):
            c.wait()
    acc_cols = []
    p_own_row = jnp.zeros((1, D_ATTN), F32)
    l_row = jnp.ones((1, D_ATTN), F32)
    for h in range(ATTN_HEADS):
        qc = q_col[h * ATTN_HEAD_DIM:(h + 1) * ATTN_HEAD_DIM]
        base = h * per_head
        s_pages = [jnp.sum(kbuf[base + g] * qc, axis=0, keepdims=True) for g in range(per_head)]
        s_own = jnp.sum(jnp.where(head_of_lane == h, qk_own, 0.0), axis=-1, keepdims=True)
        m = s_own
        for s in s_pages:
            m = jnp.maximum(m, jnp.max(s, axis=-1, keepdims=True))
        p_own = jnp.exp(s_own - m)
        l = p_own
        acc = jnp.zeros(kbuf.shape[1:], F32)
        for g, s in enumerate(s_pages):
            p = jnp.exp(s - m)
            l = l + jnp.sum(p, axis=-1, keepdims=True)
            acc = acc + vbuf[base + g] * p
        acc_cols.append(jnp.sum(acc, axis=-1, keepdims=True))
        p_own_row = jnp.where(head_of_lane == h, p_own, p_own_row)
        l_row = jnp.where(head_of_lane == h, l, l_row)
    acc_col = jnp.concatenate(acc_cols, axis=0)
    acc_row = jnp.concatenate([_col_to_row(acc_col[t * LANES:(t + 1) * LANES]) for t in range(tiles)], axis=1)
    o_ref[0] = (acc_row + p_own_row * v_row) / l_row


def _sample_attend(q, k_new, v_new, kp5, vp5, pt_flat, top_flat, layer, n_pages, n_sel, ppb):
    db = q.shape[0]
    ps = kp5.shape[-1]
    row = pl.BlockSpec((1, 1, D_ATTN), lambda d, pt, top: (d, 0, 0))
    n_buf = ATTN_HEADS * n_sel * ppb
    out = pl.pallas_call(
        functools.partial(_sample_attend_body, layer=layer, n_pages=n_pages, n_sel=n_sel, ppb=ppb),
        grid_spec=pltpu.PrefetchScalarGridSpec(
            num_scalar_prefetch=2,
            grid=(db,),
            in_specs=[row, row, row, pl.BlockSpec(memory_space=pl.ANY), pl.BlockSpec(memory_space=pl.ANY)],
            out_specs=row,
            scratch_shapes=[pltpu.VMEM((n_buf, ATTN_HEAD_DIM, ps), F32), pltpu.VMEM((n_buf, ATTN_HEAD_DIM, ps), F32),
                            pltpu.SemaphoreType.DMA((2,))],
        ),
        out_shape=jax.ShapeDtypeStruct((db, 1, D_ATTN), F32),
        compiler_params=_cparams(("arbitrary",)),
        name="sample_attend",
    )(pt_flat, top_flat, q.reshape(db, 1, D_ATTN), k_new.reshape(db, 1, D_ATTN), v_new.reshape(db, 1, D_ATTN),
      kp5, vp5)
    return out.reshape(db, D_ATTN)


def _moba_sample(q, k_new, v_new, k_pool, v_pool, page_table, layer):
    db = q.shape[0]
    ps = k_pool.shape[2]
    n_pages = page_table.shape[1]
    ppb = MOBA_BLOCK // ps
    nbp = (n_pages * ps) // MOBA_BLOCK
    assert nbp >= 1 and (n_pages - nbp * ppb) * ps == 0, "the cached rows must fill whole MoBA blocks"
    n_sel = min(MOBA_TOPK, nbp)
    kp5 = jnp.transpose(k_pool, (0, 1, 3, 4, 2))
    vp5 = jnp.transpose(v_pool, (0, 1, 3, 4, 2))
    pt_flat = page_table.reshape(-1)
    ksum = _page_sums(kp5, pt_flat, layer, db, n_pages)
    top = _sample_select(q, ksum, nbp, ppb)
    top_flat = top[:, :, :n_sel].reshape(-1)
    return _sample_attend(q, k_new, v_new, kp5, vp5, pt_flat, top_flat, layer, n_pages, n_sel, ppb)


def _mlstm_step_body(qk_ref, vm_ref, if_ref, conv_ref, cw_ref, c_ref, n_ref, m_ref,
                     h_ref, c_out, n_out, m_out):
    tb = qk_ref.shape[0]
    w = cw_ref[...]
    lane = _iota((1, LANES), 1)

    def one(r):
        x = qk_ref[pl.ds(r, 1), :]
        cb = conv_ref[pl.ds(r, 1)][0]
        y = cb[0:1] * w[0:1]
        for j in range(1, CONV_W - 1):
            y = y + cb[j:j + 1] * w[j:j + 1]
        y = y + x * w[CONV_W - 1:CONV_W]
        qkc = _silu(y)
        gates = if_ref[pl.ds(r, 1), :]
        m_all = m_ref[pl.ds(r, 1), :]
        v_all = vm_ref[pl.ds(r, 1), :]
        for h in range(MLSTM_HEADS):
            sl = slice(h * MLSTM_HEAD_DIM, (h + 1) * MLSTM_HEAD_DIM)
            q = qkc[:, sl]
            k = qkc[:, D_MLSTM + h * MLSTM_HEAD_DIM:D_MLSTM + (h + 1) * MLSTM_HEAD_DIM] * (MLSTM_HEAD_DIM ** -0.5)
            v = v_all[:, sl]
            ig = gates[:, h:h + 1]
            fg = gates[:, MLSTM_HEADS + h:MLSTM_HEADS + h + 1]
            m_prev = m_all[:, h:h + 1]
            inter = _log_sigmoid(fg) + m_prev
            m_t = jnp.maximum(inter, ig)
            wgt = jnp.exp(ig - m_t)
            w0 = jnp.exp(inter - m_t)
            c_prev = c_ref[pl.ds(r, 1), h][0]
            n_prev = n_ref[pl.ds(r, 1), h]
            sqk = jnp.sum(q * k, axis=-1, keepdims=True) * wgt
            cq = _nt(jnp.broadcast_to(q, (SUBLANES, MLSTM_HEAD_DIM)), c_prev, precision=HI)[0:1]
            num = w0 * cq + sqk * v
            den = w0 * jnp.sum(n_prev * q, axis=-1, keepdims=True) + sqk
            h_ref[pl.ds(r, 1), sl] = num / jnp.maximum(jnp.abs(den), jnp.exp(-m_t))
            c_out[pl.ds(r, 1), h] = (w0 * c_prev + (wgt * _row_to_col(v)) * k)[None]
            n_out[pl.ds(r, 1), h] = w0 * n_prev + wgt * k
            m_all = jnp.where(lane == h, m_t, m_all)
        m_out[pl.ds(r, 1), :] = m_all

    for r in range(tb):
        one(r)


def _mlstm_step(qk_m, v_m, if_m, conv_state, conv_w, c, n, m_pad, tb):
    db = qk_m.shape[0]
    row = lambda w: pl.BlockSpec((tb, w), lambda i: (i, 0))
    c_spec = pl.BlockSpec((tb, MLSTM_HEADS, MLSTM_HEAD_DIM, MLSTM_HEAD_DIM), lambda i: (i, 0, 0, 0))
    n_spec = pl.BlockSpec((tb, MLSTM_HEADS, MLSTM_HEAD_DIM), lambda i: (i, 0, 0))
    return pl.pallas_call(
        _mlstm_step_body,
        grid=(db // tb,),
        in_specs=[row(2 * D_MLSTM), row(D_MLSTM), row(LANES),
                  pl.BlockSpec((tb, CONV_W - 1, 2 * D_MLSTM), lambda i: (i, 0, 0)),
                  pl.BlockSpec((CONV_W, 2 * D_MLSTM), lambda i: (0, 0)), c_spec, n_spec, row(LANES)],
        out_specs=[row(D_MLSTM), c_spec, n_spec, row(LANES)],
        out_shape=[jax.ShapeDtypeStruct((db, D_MLSTM), F32), jax.ShapeDtypeStruct(c.shape, F32),
                   jax.ShapeDtypeStruct(n.shape, F32), jax.ShapeDtypeStruct((db, LANES), F32)],
        compiler_params=_cparams(("arbitrary",)),
        name="mlstm_step",
    )(qk_m, v_m, if_m, conv_state, conv_w, c, n, m_pad)


def _sample_path(x, mod6, p, k_pool, v_pool, page_table, c0, n0, m0, conv0, layer):
    db, t, d = x.shape
    assert t == 1, "one new token per sequence"
    assert db % SUBLANES == 0
    xs = x.reshape(1, db, d)
    sh1, sc1, g1, sh2, sc2, g2 = [m.reshape(1, db, d) for m in mod6]
    q, k, v, _, _, _, qk_m, v_m, o_m, if_m, _ = _front(xs, sc1, sh1, p["norm1_g"], p["w_in"], p["b_if"], db)
    q, k, v, qk_m, v_m, if_m = [a[0] for a in (q, k, v, qk_m, v_m, if_m)]
    attn = _moba_sample(q, k, v, k_pool, v_pool, page_table, layer)
    m_pad = jnp.pad(m0, ((0, 0), (0, LANES - MLSTM_HEADS)))
    h_s, c_s, n_s, m_s = _mlstm_step(qk_m, v_m, if_m, conv0, p["conv_w"], c0, n0, m_pad, SUBLANES)
    y = _layer_tail(xs, (g1, sh2, sc2, g2), attn[None], h_s[None], o_m, p["attn_norm_g"], p["mlstm_norm_g"],
                    p["w_out"], p["norm2_g"], p["peer_w_query"], p["peer_sub_keys"], p["uv"], p["normf_g"],
                    db, min(64, db))
    conv_s = jnp.concatenate([conv0[:, 1:], qk_m[:, None, :]], axis=1)
    return y.reshape(db, t, d), k, v, c_s, n_s, m_s[:, :MLSTM_HEADS], conv_s


def kernel(x_prompt, x_sample, cache_k_pool, cache_v_pool, page_table, state_mlstm_C, state_mlstm_n, state_mlstm_m, state_conv, c_prompt, c_sample, w_ada, b_ada, norm1_g, w_in, b_if, conv_w, attn_norm_g, mlstm_norm_g, w_out, norm2_g, peer_w_query, peer_sub_keys, peer_u, peer_v, normf_g):
    b, s, d = x_prompt.shape
    db = x_sample.shape[0]
    l = 0
    p = dict(norm1_g=norm1_g[l], w_in=w_in[l], b_if=b_if[l], conv_w=conv_w[l], attn_norm_g=attn_norm_g[l],
             mlstm_norm_g=mlstm_norm_g[l], w_out=w_out[l], norm2_g=norm2_g[l], peer_w_query=peer_w_query[l],
             peer_sub_keys=peer_sub_keys[l], normf_g=normf_g,
             uv=jnp.concatenate([_pack_bf16_pairs(peer_u[l]), _pack_bf16_pairs(peer_v[l])], axis=1)[:, None, :])
    rows = b + db
    pad = -rows % SUBLANES
    c_all = jnp.pad(jnp.concatenate([c_prompt, c_sample], axis=0), ((0, pad), (0, 0)))
    mod = _ada(c_all, w_ada[l], b_ada[l])
    mod_p = [mod[:b, i * d:(i + 1) * d] for i in range(6)]
    mod_s = [mod[b:rows, i * d:(i + 1) * d] for i in range(6)]
    y_p, k_p, v_p, c_p, n_p, m_p, conv_p = _prompt_path(x_prompt, mod_p, p)
    y_s, k_s, v_s, c_s, n_s, m_s, conv_s = _sample_path(
        x_sample, mod_s, p, cache_k_pool, cache_v_pool, page_table,
        state_mlstm_C[l], state_mlstm_n[l], state_mlstm_m[l], state_conv[l], l)
    hp = (b, s, ATTN_HEADS, ATTN_HEAD_DIM)
    hs = (db, 1, ATTN_HEADS, ATTN_HEAD_DIM)
    return (y_p, y_s, k_p.reshape(hp)[None], v_p.reshape(hp)[None], k_s.reshape(hs)[None], v_s.reshape(hs)[None],
            c_p[None], n_p[None], m_p[None], conv_p[None], c_s[None], n_s[None], m_s[None], conv_s[None])
```

```python
import functools

import jax
import jax.numpy as jnp
from jax import lax
from jax.experimental import pallas as pl
from jax.experimental.pallas import tpu as pltpu

F32 = jnp.float32
BF16 = jnp.bfloat16
I32 = jnp.int32
HI = lax.Precision.HIGHEST

SUBLANES = 8
LANES = 128
VMEM_LIMIT_BYTES = 56 * 1024 * 1024

D_MODEL = 1024
D_ATTN = 512
D_MLSTM = 512
ATTN_HEADS = 8
ATTN_HEAD_DIM = 64
MLSTM_HEADS = 4
MLSTM_HEAD_DIM = 128
MOBA_BLOCK = 256
MOBA_TOPK = 3
CONV_W = 4
PEER_HEADS = 8
PEER_N_KEYS = 128
PEER_TOPK = 16
PEER_HALF = 128
EPS = 1e-6
N_GATES = 2 * MLSTM_HEADS
C_Q, C_K, C_V, C_QK, C_VM, C_OM, C_IF = 0, 512, 1024, 1536, 2560, 3072, 3584
NEG = -1e30


def _cparams(sem):
    return pltpu.CompilerParams(dimension_semantics=sem, vmem_limit_bytes=VMEM_LIMIT_BYTES)


def _nt(a, b, **kw):
    return lax.dot_general(a, b, (((1,), (1,)), ((), ())), preferred_element_type=F32, **kw)


def _tn(a, b, **kw):
    return lax.dot_general(a, b, (((0,), (0,)), ((), ())), preferred_element_type=F32, **kw)


def _silu(x):
    return x * jax.nn.sigmoid(x)


def _log_sigmoid(x):
    return jnp.minimum(x, 0.0) - jnp.log1p(jnp.exp(-jnp.abs(x)))


def _iota(shape, dim):
    return lax.broadcasted_iota(I32, shape, dim)


def _ada_body(c_ref, w_ref, b_ref, o_ref):
    o_ref[...] = jnp.dot(_silu(c_ref[...]), w_ref[...], precision=HI, preferred_element_type=F32) + b_ref[...]


def _ada(c, w_ada, b_ada):
    rows, d = c.shape
    n = w_ada.shape[1]
    tn = 512
    return pl.pallas_call(
        _ada_body,
        grid=(n // tn,),
        in_specs=[pl.BlockSpec((rows, d), lambda j: (0, 0)),
                  pl.BlockSpec((d, tn), lambda j: (0, j)),
                  pl.BlockSpec((1, tn), lambda j: (0, j))],
        out_specs=pl.BlockSpec((rows, tn), lambda j: (0, j)),
        out_shape=jax.ShapeDtypeStruct((rows, n), F32),
        compiler_params=_cparams(("arbitrary",)),
        name="ada_mod",
    )(c, w_ada, b_ada.reshape(1, n))


def _front_body(x_ref, sc_ref, sh_ref, g_ref, w_ref, wif_ref, wift_ref, bif_ref, bift_ref,
                q_ref, k_ref, v_ref, qb_ref, kb_ref, vb_ref, qk_ref, vm_ref, om_ref, if_ref, ift_ref):
    x = x_ref[0]
    h = x * lax.rsqrt(jnp.mean(x * x, axis=-1, keepdims=True) + EPS) * g_ref[...]
    h = h * (1.0 + sc_ref[0]) + sh_ref[0]
    hb = h.astype(BF16)

    def seg(lo, hi):
        return jnp.dot(hb, w_ref[:, lo:hi], preferred_element_type=F32)

    q = seg(C_Q, C_K)
    q_ref[0] = q
    qb_ref[0] = (q * (ATTN_HEAD_DIM ** -0.5)).astype(BF16)
    k = seg(C_K, C_V)
    k_ref[0] = k
    kb_ref[0] = k.astype(BF16)
    v = seg(C_V, C_QK)
    v_ref[0] = v
    vb_ref[0] = v.astype(BF16)
    qk_ref[0] = seg(C_QK, C_VM)
    vm_ref[0] = seg(C_VM, C_OM)
    om_ref[0] = seg(C_OM, C_IF)
    if_ref[0] = jnp.dot(h, wif_ref[...], precision=HI, preferred_element_type=F32) + bif_ref[...]
    ift_ref[0] = _nt(wift_ref[...], h, precision=HI) + bift_ref[...]


def _front(x, sc, sh, norm_g, w_in, b_if, tm):
    b, s, d = x.shape
    per_row = sc.shape[1] != 1
    mod_spec = (pl.BlockSpec((1, tm, d), lambda bi, i: (bi, i, 0)) if per_row
                else pl.BlockSpec((1, 1, d), lambda bi, i: (bi, 0, 0)))
    w_main = w_in[:, :C_IF].astype(BF16)
    w_if = jnp.pad(w_in[:, C_IF:], ((0, 0), (0, LANES - N_GATES)))
    w_ift = w_in[:, C_IF:].T
    b_row = jnp.pad(b_if, (0, LANES - N_GATES)).reshape(1, LANES)
    b_col = b_if.reshape(N_GATES, 1)
    const = lambda shape: pl.BlockSpec(shape, lambda bi, i: (0,) * len(shape))
    tok = lambda w: pl.BlockSpec((1, tm, w), lambda bi, i: (bi, i, 0))
    out_shapes = [
        jax.ShapeDtypeStruct((b, s, D_ATTN), F32),
        jax.ShapeDtypeStruct((b, s, D_ATTN), F32),
        jax.ShapeDtypeStruct((b, s, D_ATTN), F32),
        jax.ShapeDtypeStruct((b, s, D_ATTN), BF16),
        jax.ShapeDtypeStruct((b, s, D_ATTN), BF16),
        jax.ShapeDtypeStruct((b, s, D_ATTN), BF16),
        jax.ShapeDtypeStruct((b, s, 2 * D_MLSTM), F32),
        jax.ShapeDtypeStruct((b, s, D_MLSTM), F32),
        jax.ShapeDtypeStruct((b, s, D_MLSTM), F32),
        jax.ShapeDtypeStruct((b, s, LANES), F32),
        jax.ShapeDtypeStruct((b, N_GATES, s), F32),
    ]
    out_specs = [tok(D_ATTN)] * 6 + [tok(2 * D_MLSTM), tok(D_MLSTM), tok(D_MLSTM), tok(LANES),
                                     pl.BlockSpec((1, N_GATES, tm), lambda bi, i: (bi, 0, i))]
    return pl.pallas_call(
        _front_body,
        grid=(b, s // tm),
        in_specs=[tok(d), mod_spec, mod_spec, const((1, d)), const((d, C_IF)), const((d, LANES)),
                  const((N_GATES, d)), const((1, LANES)), const((N_GATES, 1))],
        out_specs=out_specs,
        out_shape=out_shapes,
        compiler_params=_cparams(("arbitrary", "arbitrary")),
        name="front",
    )(x, sc, sh, norm_g.reshape(1, d), w_main, w_if, w_ift, b_row, b_col)


def _km_body(k_ref, km_ref, *, hb):
    n = pl.program_id(1)

    @pl.when(n == 0)
    def _():
        km_ref[...] = jnp.zeros_like(km_ref)

    lw = km_ref.shape[2]
    ones = jnp.full((MOBA_BLOCK, lw), 1.0 / MOBA_BLOCK, F32)
    mean_b = _tn(k_ref[0], ones, precision=HI)
    row = _iota((D_ATTN, lw), 0)
    col = _iota((D_ATTN, lw), 1)
    km_ref[0] += jnp.where(col == (row // ATTN_HEAD_DIM) * hb + n, mean_b, 0.0)


def _block_means(k, nb, hb):
    lw = ATTN_HEADS * hb
    b = k.shape[0]
    return pl.pallas_call(
        functools.partial(_km_body, hb=hb),
        grid=(b, nb),
        in_specs=[pl.BlockSpec((1, MOBA_BLOCK, D_ATTN), lambda bi, n: (bi, n, 0))],
        out_specs=pl.BlockSpec((1, D_ATTN, lw), lambda bi, n: (bi, 0, 0)),
        out_shape=jax.ShapeDtypeStruct((b, D_ATTN, lw), F32),
        compiler_params=_cparams(("arbitrary", "arbitrary")),
        name="moba_block_means",
    )(k)


def _sel_body(q_ref, km_ref, o_ref, *, hb, n_top):
    own = pl.program_id(1)
    s = jnp.dot(q_ref[0], km_ref[0], precision=HI, preferred_element_type=F32)
    shape = s.shape
    lw = shape[1]
    lane = _iota(shape, 1)
    blk = lane & (hb - 1)
    head = lane >> (hb.bit_length() - 1)
    valid = (blk < own) & (head < ATTN_HEADS)
    sel = jnp.zeros(shape, jnp.bool_)
    for h in range(ATTN_HEADS):
        sm = jnp.where(valid & (head == h), s, -jnp.inf)
        for _ in range(n_top):
            m = jnp.max(sm, axis=-1, keepdims=True)
            i = jnp.min(jnp.where(sm == m, lane, lw), axis=-1, keepdims=True)
            hit = lane == i
            sel = sel | (hit & (m > -jnp.inf))
            sm = jnp.where(hit, -jnp.inf, sm)
    o_ref[0] = jnp.where(sel, 0.0, NEG)


def _moba_select(q, km, nb, hb):
    b, s, _ = q.shape
    lw = km.shape[2]
    return pl.pallas_call(
        functools.partial(_sel_body, hb=hb, n_top=min(MOBA_TOPK, nb)),
        grid=(b, s // MOBA_BLOCK),
        in_specs=[pl.BlockSpec((1, MOBA_BLOCK, D_ATTN), lambda bi, i: (bi, i, 0)),
                  pl.BlockSpec((1, D_ATTN, lw), lambda bi, i: (bi, 0, 0))],
        out_specs=pl.BlockSpec((1, MOBA_BLOCK, lw), lambda bi, i: (bi, i, 0)),
        out_shape=jax.ShapeDtypeStruct((b, s, lw), F32),
        compiler_params=_cparams(("arbitrary", "arbitrary")),
        name="moba_select",
    )(q, km)


def _attn_body(q_ref, k_ref, v_ref, sb_ref, lid_ref, o_ref, *, hb):
    own = pl.program_id(2)
    tq = MOBA_BLOCK
    tiles = q_ref.shape[2] // LANES
    n_heads = 2 * tiles
    lane = _iota((tq, LANES), 1)
    low = lane < ATTN_HEAD_DIM
    sb = sb_ref[0]
    one = jnp.ones((tq, LANES), BF16)
    zero = jnp.zeros((tq, LANES), BF16)
    lane_b = lid_ref[...]
    low_b = lane_b < ATTN_HEAD_DIM
    qs, mine, bias0 = [], [], []
    for hh in range(n_heads):
        t = hh // 2
        q = q_ref[0, :, t * LANES:(t + 1) * LANES]
        is_low = hh % 2 == 0
        first = ATTN_HEAD_DIM if is_low else 0
        tile, off = divmod(hh * hb, LANES)
        bias = sb[:, tile * LANES:(tile + 1) * LANES]
        shift = (first - off) % LANES
        if shift:
            bias = pltpu.roll(bias, shift, 1)
        keep = low_b if is_low else jnp.logical_not(low_b)
        qs.append(jnp.where(keep, q, bias.astype(BF16)))
        mine.append(keep)
        bias0.append(first)
    row = _iota((tq, tq), 0)
    col = _iota((tq, tq), 1)

    def attend(j, state, is_own):
        start = pl.multiple_of(j * tq, tq)
        new = []
        for hh in range(n_heads):
            sl = slice((hh // 2) * LANES, (hh // 2 + 1) * LANES)
            kj = k_ref[0, pl.ds(start, tq), sl]
            vj = v_ref[0, pl.ds(start, tq), sl]
            m, acc = state[hh]
            if is_own:
                s = _nt(qs[hh], jnp.where(mine[hh], kj, zero))
                s = jnp.where(col <= row, s, -jnp.inf)
            else:
                hot = lane_b == (bias0[hh] + j).astype(F32).astype(BF16)
                s = _nt(qs[hh], jnp.where(mine[hh], kj, jnp.where(hot, one, zero)))
            m_new = jnp.maximum(m, jnp.max(s, axis=-1, keepdims=True))
            p = jnp.exp(s - m_new)
            acc = jnp.exp(m - m_new) * acc + jnp.dot(p.astype(BF16), jnp.where(mine[hh], vj, one),
                                                      preferred_element_type=F32)
            new.append((m_new, acc))
        return tuple(new)

    init = tuple((jnp.full((tq, 1), -jnp.inf, F32), jnp.zeros((tq, LANES), F32)) for _ in range(n_heads))
    state = attend(own, init, True)
    state = lax.fori_loop(0, own, lambda j, st: attend(j, st, False), state)
    for t in range(tiles):
        outs = []
        for hh in (2 * t, 2 * t + 1):
            acc = state[hh][1]
            outs.append(acc / pltpu.roll(acc, ATTN_HEAD_DIM, 1))
        o_ref[0, :, t * LANES:(t + 1) * LANES] = jnp.where(low, outs[0], outs[1])


ATTN_LANES_PER_STEP = 2 * LANES


def _moba_attend(qb, kb, vb, selbias, hb):
    b, s, _ = qb.shape
    gw = ATTN_LANES_PER_STEP
    groups = D_ATTN // gw
    sw = selbias.shape[2] // groups
    assert sw % LANES == 0
    return pl.pallas_call(
        functools.partial(_attn_body, hb=hb),
        grid=(b, groups, s // MOBA_BLOCK),
        in_specs=[pl.BlockSpec((1, MOBA_BLOCK, gw), lambda bi, p, i: (bi, i, p)),
                  pl.BlockSpec((1, s, gw), lambda bi, p, i: (bi, 0, p)),
                  pl.BlockSpec((1, s, gw), lambda bi, p, i: (bi, 0, p)),
                  pl.BlockSpec((1, MOBA_BLOCK, sw), lambda bi, p, i: (bi, i, p)),
                  pl.BlockSpec((MOBA_BLOCK, LANES), lambda bi, p, i: (0, 0))],
        out_specs=pl.BlockSpec((1, MOBA_BLOCK, gw), lambda bi, p, i: (bi, i, p)),
        out_shape=jax.ShapeDtypeStruct((b, s, D_ATTN), F32),
        compiler_params=_cparams(("arbitrary", "arbitrary", "arbitrary")),
        name="moba_attend",
    )(qb, kb, vb, selbias, jnp.broadcast_to(jnp.arange(LANES, dtype=F32).astype(BF16), (MOBA_BLOCK, LANES)))


MIN_HEAD_STRIDE = 32


def _moba_prompt(q, k, qb, kb, vb):
    s = q.shape[1]
    assert s % MOBA_BLOCK == 0
    nb = s // MOBA_BLOCK
    assert nb & (nb - 1) == 0, "block count must be a power of two"
    hb = max(nb, MIN_HEAD_STRIDE)
    assert hb <= ATTN_HEAD_DIM, "the per-block bias must fit in the free half of a lane tile"
    km = _block_means(k, nb, hb)
    selbias = _moba_select(q, km, nb, hb)
    return _moba_attend(qb, kb, vb, selbias, hb)


def _mlstm_body(qk_ref, vm_ref, if_ref, ift_ref, cw_ref, h_ref, c_ref, n_ref, m_ref, prev_scr):
    i = pl.program_id(1)
    L = qk_ref.shape[1]

    @pl.when(i == 0)
    def _():
        prev_scr[...] = jnp.zeros_like(prev_scr)
        c_ref[...] = jnp.zeros_like(c_ref)
        n_ref[...] = jnp.zeros_like(n_ref)
        m_ref[...] = jnp.zeros_like(m_ref)

    x = qk_ref[0]
    prev = prev_scr[...]
    w = cw_ref[...]
    rows = _iota(x.shape, 0)
    y = x * w[CONV_W - 1:CONV_W]
    for sft in range(1, CONV_W):
        xs = jnp.where(rows >= sft, pltpu.roll(x, sft, 0), pltpu.roll(prev, sft, 0))
        y = y + xs * w[CONV_W - 1 - sft:CONV_W - sft]
    prev_scr[...] = x
    qkc = _silu(y)

    gcol = if_ref[0]
    grow = ift_ref[0]
    tri_r = _iota((L, L), 0)
    tri_c = _iota((L, L), 1)
    lower = tri_c <= tri_r
    b_col = jnp.dot(lower.astype(F32), _log_sigmoid(gcol), precision=HI, preferred_element_type=F32)
    b_row = jnp.dot(_log_sigmoid(grow), (tri_r <= tri_c).astype(F32), precision=HI, preferred_element_type=F32)
    m_all = m_ref[0]
    lane = _iota(m_all.shape, 1)
    for h in range(MLSTM_HEADS):
        sl = slice(h * MLSTM_HEAD_DIM, (h + 1) * MLSTM_HEAD_DIM)
        qh = qkc[:, sl]
        kh = qkc[:, D_MLSTM + h * MLSTM_HEAD_DIM:D_MLSTM + (h + 1) * MLSTM_HEAD_DIM] * (MLSTM_HEAD_DIM ** -0.5)
        vh = vm_ref[0][:, sl]
        fh = MLSTM_HEADS + h
        bc = b_col[:, fh:fh + 1]
        br = b_row[fh:fh + 1, :]
        ic = gcol[:, h:h + 1]
        ir = grow[h:h + 1, :]
        m_prev = m_all[:, h:h + 1]
        c_prev = c_ref[0, h]
        n_prev = n_ref[0, h:h + 1, :]
        dmat = jnp.where(lower, bc - br + ir, -jnp.inf)
        inter = bc + m_prev
        m_t = jnp.maximum(inter, jnp.max(dmat, axis=-1, keepdims=True))
        wmat = jnp.exp(dmat - m_t)
        w0 = jnp.exp(inter - m_t)
        qb, kb, vb = qh.astype(BF16), kh.astype(BF16), vh.astype(BF16)
        sqk = _nt(qb, kb) * wmat
        num = w0 * _nt(qb, c_prev.astype(BF16)) + jnp.dot(sqk.astype(BF16), vb, preferred_element_type=F32)
        den = w0 * jnp.sum(qh * n_prev, axis=-1, keepdims=True) + jnp.sum(sqk, axis=-1, keepdims=True)
        h_ref[0, :, sl] = num / jnp.maximum(jnp.abs(den), jnp.exp(-m_t))
        m_new = m_t[L - 1:L, :]
        b_last = bc[L - 1:L, :]
        g = jnp.exp(b_last - bc + ic - m_new)
        decay = jnp.exp(b_last + m_prev - m_new)
        c_ref[0, h] = decay * c_prev + _tn((g * vh).astype(BF16), kb)
        n_ref[0, h:h + 1, :] = decay * n_prev + jnp.sum(g * kh, axis=0, keepdims=True)
        m_all = jnp.where(lane == h, m_new, m_all)
    m_ref[0] = m_all


def _mlstm_prompt(qk_m, v_m, if_m, if_t, conv_w, chunk):
    b, s, _ = qk_m.shape
    tok = lambda w: pl.BlockSpec((1, chunk, w), lambda bi, i: (bi, i, 0))
    return pl.pallas_call(
        _mlstm_body,
        grid=(b, s // chunk),
        in_specs=[tok(2 * D_MLSTM), tok(D_MLSTM), tok(LANES),
                  pl.BlockSpec((1, N_GATES, chunk), lambda bi, i: (bi, 0, i)),
                  pl.BlockSpec((CONV_W, 2 * D_MLSTM), lambda bi, i: (0, 0))],
        out_specs=[tok(D_MLSTM),
                   pl.BlockSpec((1, MLSTM_HEADS, MLSTM_HEAD_DIM, MLSTM_HEAD_DIM), lambda bi, i: (bi, 0, 0, 0)),
                   pl.BlockSpec((1, MLSTM_HEADS, MLSTM_HEAD_DIM), lambda bi, i: (bi, 0, 0)),
                   pl.BlockSpec((1, 1, LANES), lambda bi, i: (bi, 0, 0))],
        out_shape=[jax.ShapeDtypeStruct((b, s, D_MLSTM), F32),
                   jax.ShapeDtypeStruct((b, MLSTM_HEADS, MLSTM_HEAD_DIM, MLSTM_HEAD_DIM), F32),
                   jax.ShapeDtypeStruct((b, MLSTM_HEADS, MLSTM_HEAD_DIM), F32),
                   jax.ShapeDtypeStruct((b, 1, LANES), F32)],
        scratch_shapes=[pltpu.VMEM((chunk, 2 * D_MLSTM), F32)],
        compiler_params=_cparams(("arbitrary", "arbitrary")),
        name="mlstm_prompt",
    )(qk_m, v_m, if_m, if_t, conv_w)


def _back_body(x_ref, a_ref, hm_ref, om_ref, g1_ref, sc_ref, sh_ref, ag_ref, mg_ref, wo_ref, n2_ref, wq_ref,
               x1_ref, h2_ref, pq_ref):
    a = a_ref[0]
    tm = a.shape[0]
    lane = _iota((tm, LANES), 1)
    low = lane < ATTN_HEAD_DIM
    proj = jnp.zeros((tm, D_MODEL), F32)
    for t in range(D_ATTN // LANES):
        sl = slice(t * LANES, (t + 1) * LANES)
        at = a[:, sl]
        sq = at * at
        s_lo = jnp.sum(jnp.where(low, sq, 0.0), axis=-1, keepdims=True)
        s_hi = jnp.sum(jnp.where(low, 0.0, sq), axis=-1, keepdims=True)
        scale = jnp.where(low, lax.rsqrt(s_lo / ATTN_HEAD_DIM + EPS), lax.rsqrt(s_hi / ATTN_HEAD_DIM + EPS))
        an = at * scale * ag_ref[:, sl]
        proj = proj + jnp.dot(an.astype(BF16), wo_ref[sl, :], preferred_element_type=F32)
    hm = hm_ref[0]
    om = om_ref[0]
    for t in range(MLSTM_HEADS):
        sl = slice(t * LANES, (t + 1) * LANES)
        ht = hm[:, sl]
        hn = ht * lax.rsqrt(jnp.mean(ht * ht, axis=-1, keepdims=True) + EPS) * mg_ref[:, sl]
        mm = jax.nn.sigmoid(om[:, sl]) * hn
        proj = proj + jnp.dot(mm.astype(BF16), wo_ref[D_ATTN + t * LANES:D_ATTN + (t + 1) * LANES, :],
                              preferred_element_type=F32)
    x1 = x_ref[0] + g1_ref[0] * proj
    x1_ref[0] = x1
    h2 = x1 * lax.rsqrt(jnp.mean(x1 * x1, axis=-1, keepdims=True) + EPS) * n2_ref[...]
    h2 = h2 * (1.0 + sc_ref[0]) + sh_ref[0]
    h2_ref[0] = h2
    pq_ref[0] = jnp.dot(h2.astype(BF16), wq_ref[...], preferred_element_type=F32)


def _back(x, attn, h_m, o_m, g1, sc2, sh2, attn_g, mlstm_g, w_out, norm2_g, w_query, tm):
    b, s, d = x.shape
    per_row = g1.shape[1] != 1
    mod_spec = (pl.BlockSpec((1, tm, d), lambda bi, i: (bi, i, 0)) if per_row
                else pl.BlockSpec((1, 1, d), lambda bi, i: (bi, 0, 0)))
    nq = w_query.shape[1]
    const = lambda shape: pl.BlockSpec(shape, lambda bi, i: (0,) * len(shape))
    tok = lambda w: pl.BlockSpec((1, tm, w), lambda bi, i: (bi, i, 0))
    return pl.pallas_call(
        _back_body,
        grid=(b, s // tm),
        in_specs=[tok(d), tok(D_ATTN), tok(D_MLSTM), tok(D_MLSTM), mod_spec, mod_spec, mod_spec,
                  const((1, D_ATTN)), const((1, D_MLSTM)), const((d, d)), const((1, d)), const((d, nq))],
        out_specs=[tok(d), tok(d), tok(nq)],
        out_shape=[jax.ShapeDtypeStruct((b, s, d), F32), jax.ShapeDtypeStruct((b, s, d), F32),
                   jax.ShapeDtypeStruct((b, s, nq), F32)],
        compiler_params=_cparams(("arbitrary", "arbitrary")),
        name="back",
    )(x, attn, h_m, o_m, g1, sc2, sh2, attn_g.reshape(1, D_ATTN), mlstm_g.reshape(1, D_MLSTM),
      w_out.astype(BF16), norm2_g.reshape(1, d), w_query.astype(BF16))


def _peer_route_body(pq_ref, sk_ref, idx_ref, gate_ref):
    tp = pq_ref.shape[0]
    k = PEER_TOPK
    n_exp = PEER_N_KEYS * PEER_N_KEYS
    out_row = _iota((k, tp), 0)

    def extract(score, tag, fill):
        vals = jnp.zeros((k, tp), F32)
        tags = jnp.zeros((k, tp), F32)
        for r in range(k):
            m = jnp.max(score, axis=0, keepdims=True)
            t = jnp.min(jnp.where(score == m, tag, fill), axis=0, keepdims=True)
            vals = jnp.where(out_row == r, m, vals)
            tags = jnp.where(out_row == r, t, tags)
            score = jnp.where(tag == t, -jnp.inf, score)
        return vals, tags

    key_row = _iota((PEER_N_KEYS, tp), 0).astype(F32)
    pos = _iota((k * k, tp), 0).astype(F32)
    gates, ids = [], []
    for h in range(PEER_HEADS):
        halves = []
        for p in range(2):
            c0 = (2 * h + p) * PEER_HALF
            qh = pq_ref[:, c0:c0 + PEER_HALF].astype(BF16)
            s = _nt(sk_ref[2 * h + p].astype(BF16), qh)
            halves.append(extract(s, key_row, float(PEER_N_KEYS)))
        (v0, i0), (v1, i1) = halves
        cand = jnp.concatenate([v0[a:a + 1] + v1 for a in range(k)], axis=0)
        eid = jnp.concatenate([i0[a:a + 1] * PEER_N_KEYS + i1 for a in range(k)], axis=0)
        comb = pos * n_exp + eid
        c_s, c_key = extract(cand, comb, float(k * k * n_exp))
        e = jnp.exp(c_s - c_s[0:1])
        gates.append(e / jnp.sum(e, axis=0, keepdims=True))
        ids.append(c_key)
    gate_ref[...] = jnp.concatenate(gates, axis=0).T
    keys = jnp.concatenate(ids, axis=0).T.astype(I32)
    idx_ref[...] = keys & (n_exp - 1)


def _peer_route(pq, sub_keys):
    n = pq.shape[0]
    tp = LANES
    n_pad = -(-n // tp) * tp
    pq = jnp.pad(pq, ((0, n_pad - n), (0, 0)))
    sk = sub_keys.reshape(PEER_HEADS * 2, PEER_N_KEYS, PEER_HALF)
    idx, gate = pl.pallas_call(
        _peer_route_body,
        grid=(n_pad // tp,),
        in_specs=[pl.BlockSpec((tp, pq.shape[1]), lambda i: (i, 0)),
                  pl.BlockSpec(sk.shape, lambda i: (0, 0, 0))],
        out_specs=[pl.BlockSpec((tp, LANES), lambda i: (i, 0)), pl.BlockSpec((tp, LANES), lambda i: (i, 0))],
        out_shape=[jax.ShapeDtypeStruct((n_pad, LANES), I32), jax.ShapeDtypeStruct((n_pad, LANES), F32)],
        compiler_params=_cparams(("arbitrary",)),
        name="peer_route",
    )(pq, sk)
    return idx[:n], gate[:n]


N_SEL = PEER_HEADS * PEER_TOPK
EXPERT_SLOTS = 4


def _pack_bf16_pairs(a):
    half = a.shape[1] // 2
    b = lax.bitcast_convert_type(a.astype(BF16), jnp.uint16).astype(jnp.uint32)
    return (b[:, :half] << 16) | b[:, half:]


def _peer_expert_body(idx_ref, gate_ref, h2_ref, x1_ref, g2_ref, nf_ref, uv_ref, y_ref, buf, sem, peer_scr):
    tt = h2_ref.shape[1]
    row_w = uv_ref.shape[2]
    half = D_MODEL // 2

    def issue(t, slot):
        for k in range(N_SEL):
            e = idx_ref[0, 0, t * N_SEL + k]
            pltpu.make_async_copy(uv_ref.at[e], buf.at[slot, pl.ds(k, 1)],
                                  sem.at[slot]).start(priority=k % 2)

    def wait_all(slot):
        pltpu.make_async_copy(uv_ref.at[pl.ds(0, N_SEL), 0], buf.at[slot], sem.at[slot]).wait()

    rr = _iota((N_SEL, LANES), 0)
    cc = _iota((N_SEL, LANES), 1)
    diag = rr == cc

    def unpack(words):
        hi = lax.bitcast_convert_type(words & jnp.uint32(0xFFFF0000), F32)
        lo = lax.bitcast_convert_type(words << 16, F32)
        return hi, lo

    def compute(t, slot):
        x = h2_ref[0, pl.ds(t, 1), :]
        u_hi, u_lo = unpack(buf[slot, :, 0:half])
        s = jnp.sum(u_hi * x[:, 0:half] + u_lo * x[:, half:D_MODEL], axis=-1, keepdims=True)
        act = 0.5 * s * (1.0 + lax.erf(s * (0.5 ** 0.5)))
        g_row = gate_ref[0, pl.ds(t, 1), :]
        g_col = jnp.sum(jnp.where(diag, jnp.broadcast_to(g_row, (N_SEL, LANES)), 0.0), axis=-1, keepdims=True)
        w = g_col * act
        v_hi, v_lo = unpack(buf[slot, :, half:2 * half])
        peer_scr[pl.ds(t, 1), 0:half] = jnp.sum(v_hi * w, axis=0, keepdims=True)
        peer_scr[pl.ds(t, 1), half:D_MODEL] = jnp.sum(v_lo * w, axis=0, keepdims=True)

    n_slots = buf.shape[0]
    ahead = n_slots - 1
    for t in range(ahead):
        issue(t, t)

    def token_group(i, carry):
        t0 = n_slots * i
        for sl in range(n_slots):
            nxt = t0 + sl + ahead

            @pl.when(nxt < tt)
            def _():
                issue(nxt, (sl + ahead) % n_slots)

            wait_all(sl)
            compute(t0 + sl, sl)
        return carry

    lax.fori_loop(0, tt // n_slots, token_group, 0)
    y = x1_ref[0] + g2_ref[0] * peer_scr[...]
    y_ref[0] = y * lax.rsqrt(jnp.mean(y * y, axis=-1, keepdims=True) + EPS) * nf_ref[...]


def _peer_experts(idx, gate, h2, x1, g2, normf_g, uv, tt):
    b, s, d = h2.shape
    n_slots = EXPERT_SLOTS
    assert tt % n_slots == 0 and tt >= n_slots and s % tt == 0
    nt = s // tt
    per_row = g2.shape[1] != 1
    mod_spec = (pl.BlockSpec((1, tt, d), lambda bi, i: (bi, i, 0)) if per_row
                else pl.BlockSpec((1, 1, d), lambda bi, i: (bi, 0, 0)))
    tok = lambda w: pl.BlockSpec((1, tt, w), lambda bi, i: (bi, i, 0))
    idx3 = idx.reshape(b * nt, 1, tt * N_SEL)
    return pl.pallas_call(
        _peer_expert_body,
        grid=(b, nt),
        in_specs=[pl.BlockSpec((1, 1, tt * N_SEL), lambda bi, i: (bi * nt + i, 0, 0), memory_space=pltpu.SMEM),
                  tok(LANES), tok(d), tok(d), mod_spec,
                  pl.BlockSpec((1, d), lambda bi, i: (0, 0)),
                  pl.BlockSpec(memory_space=pl.ANY)],
        out_specs=tok(d),
        out_shape=jax.ShapeDtypeStruct((b, s, d), F32),
        scratch_shapes=[pltpu.VMEM((n_slots, N_SEL, uv.shape[2]), jnp.uint32), pltpu.SemaphoreType.DMA((n_slots,)),
                        pltpu.VMEM((tt, d), F32)],
        compiler_params=_cparams(("arbitrary", "arbitrary")),
        name="peer_experts",
    )(idx3, gate.reshape(b, s, LANES), h2, x1, g2, normf_g.reshape(1, d), uv)


def _layer_tail(x, mods, attn, h_m, o_m, attn_g, mlstm_g, w_out, norm2_g, w_query, sub_keys, uv, normf_g, tm, tt):
    g1, sh2, sc2, g2 = mods
    b, s, d = x.shape
    x1, h2, pq = _back(x, attn, h_m, o_m, g1, sc2, sh2, attn_g, mlstm_g, w_out, norm2_g, w_query, tm)
    idx, gate = _peer_route(pq.reshape(b * s, -1), sub_keys)
    return _peer_experts(idx, gate, h2, x1, g2, normf_g, uv, tt)


def _prompt_path(x, mod6, p):
    b, s, d = x.shape
    sh1, sc1, g1, sh2, sc2, g2 = [m.reshape(b, 1, d) for m in mod6]
    tm = min(512, s)
    q, k, v, qb, kb, vb, qk_m, v_m, o_m, if_m, if_t = _front(x, sc1, sh1, p["norm1_g"], p["w_in"], p["b_if"], tm)
    attn = _moba_prompt(q, k, qb, kb, vb)
    h_m, c_p, n_p, m_p = _mlstm_prompt(qk_m, v_m, if_m, if_t, p["conv_w"], min(256, s))
    y = _layer_tail(x, (g1, sh2, sc2, g2), attn, h_m, o_m, p["attn_norm_g"], p["mlstm_norm_g"], p["w_out"],
                    p["norm2_g"], p["peer_w_query"], p["peer_sub_keys"], p["uv"], p["normf_g"],
                    min(256, s), min(64, s))
    conv_p = qk_m[:, s - (CONV_W - 1):, :]
    return y, k, v, c_p, n_p, m_p[:, 0, :MLSTM_HEADS], conv_p


PAGES_PER_STEP = 8


def _page_sum_body(pt_ref, *refs, n_in, ppb):
    del pt_ref
    g = pl.program_id(1)
    out_ref = refs[n_in]

    @pl.when(g == 0)
    def _():
        out_ref[...] = jnp.zeros_like(out_ref)

    lane = _iota((D_ATTN, LANES), 1)
    acc = out_ref[0]
    for j in range(0, n_in, ppb):
        pages = refs[j][0, 0]
        for jj in range(1, ppb):
            pages = pages + refs[j + jj][0, 0]
        block_sum = jnp.sum(pages.reshape(D_ATTN, pages.shape[-1]), axis=-1, keepdims=True)
        acc = jnp.where(lane == (g * n_in + j) // ppb, block_sum, acc)
    out_ref[0] = acc


def _page_sums(kp5, pt_flat, layer, db, n_pages, ppb):
    assert n_pages <= LANES
    n_in = min(PAGES_PER_STEP, n_pages)
    assert n_pages % n_in == 0 and n_in % ppb == 0
    ps = kp5.shape[-1]

    def page_spec(j):
        return pl.BlockSpec((1, 1, ATTN_HEADS, ATTN_HEAD_DIM, ps),
                            lambda d, g, pt: (layer, pt[d * n_pages + g * n_in + j], 0, 0, 0))

    return pl.pallas_call(
        functools.partial(_page_sum_body, n_in=n_in, ppb=ppb),
        grid_spec=pltpu.PrefetchScalarGridSpec(
            num_scalar_prefetch=1,
            grid=(db, n_pages // n_in),
            in_specs=[page_spec(j) for j in range(n_in)],
            out_specs=pl.BlockSpec((1, D_ATTN, LANES), lambda d, g, pt: (d, 0, 0)),
        ),
        out_shape=jax.ShapeDtypeStruct((db, D_ATTN, LANES), F32),
        compiler_params=_cparams(("arbitrary", "arbitrary")),
        name="sample_page_sums",
    )(pt_flat, *([kp5] * n_in))


def _sample_select_body(q_ref, ks_ref, o_ref, *, nbp):
    q = jnp.broadcast_to(q_ref[0], (ATTN_HEADS, D_ATTN))
    qbd = jnp.where(_iota(q.shape, 1) // ATTN_HEAD_DIM == _iota(q.shape, 0), q, 0.0)
    s = jnp.dot(qbd, ks_ref[0], precision=HI, preferred_element_type=F32) * (1.0 / MOBA_BLOCK)
    lane = _iota(s.shape, 1)
    s = jnp.where(lane < nbp, s, -jnp.inf)
    top = jnp.zeros(s.shape, I32)
    for r in range(min(MOBA_TOPK, nbp)):
        m = jnp.max(s, axis=-1, keepdims=True)
        i = jnp.min(jnp.where(s == m, lane, LANES), axis=-1, keepdims=True)
        top = jnp.where(lane == r, i, top)
        s = jnp.where(lane == i, -jnp.inf, s)
    o_ref[0] = top


def _sample_select(q, ksum, nbp):
    db = q.shape[0]
    return pl.pallas_call(
        functools.partial(_sample_select_body, nbp=nbp),
        grid=(db,),
        in_specs=[pl.BlockSpec((1, 1, D_ATTN), lambda d: (d, 0, 0)),
                  pl.BlockSpec((1, D_ATTN, LANES), lambda d: (d, 0, 0))],
        out_specs=pl.BlockSpec((1, ATTN_HEADS, LANES), lambda d: (d, 0, 0)),
        out_shape=jax.ShapeDtypeStruct((db, ATTN_HEADS, LANES), I32),
        compiler_params=_cparams(("arbitrary",)),
        name="sample_select",
    )(q.reshape(db, 1, D_ATTN), ksum)


def _row_to_col(row):
    eye = _iota((LANES, LANES), 0) == _iota((LANES, LANES), 1)
    return jnp.sum(jnp.where(eye, jnp.broadcast_to(row, (LANES, LANES)), 0.0), axis=-1, keepdims=True)


def _col_to_row(col):
    eye = _iota((LANES, LANES), 0) == _iota((LANES, LANES), 1)
    return jnp.sum(jnp.where(eye, jnp.broadcast_to(col, (LANES, LANES)), 0.0), axis=0, keepdims=True)


def _sample_attend_body(pt_ref, top_ref, q_ref, kn_ref, vn_ref, kp_ref, vp_ref, o_ref, kbuf, vbuf, sem,
                        *, layer, n_pages, n_sel, ppb):
    d = pl.program_id(0)
    per_head = n_sel * ppb

    def copies(h, r, j):
        blk = top_ref[(d * ATTN_HEADS + h) * n_sel + r]
        phys = pt_ref[d * n_pages + blk * ppb + j]
        slot = h * per_head + r * ppb + j
        return (pltpu.make_async_copy(kp_ref.at[layer, phys, h], kbuf.at[slot], sem.at[0]),
                pltpu.make_async_copy(vp_ref.at[layer, phys, h], vbuf.at[slot], sem.at[1]))

    todo = [(h, r, j) for h in range(ATTN_HEADS) for r in range(n_sel) for j in range(ppb)]
    for hrj in todo:
        for c in copies(*hrj):
            c.start()
    q_row = q_ref[0] * (ATTN_HEAD_DIM ** -0.5)
    k_row = kn_ref[0]
    v_row = vn_ref[0]
    tiles = D_ATTN // LANES
    q_col = jnp.concatenate([_row_to_col(q_row[:, t * LANES:(t + 1) * LANES]) for t in range(tiles)], axis=0)
    head_of_lane = _iota((1, D_ATTN), 1) // ATTN_HEAD_DIM
    qk_own = q_row * k_row
    for hrj in todo:
        for c in copies(*hrj):
            c.wait()
    acc_cols = []
    p_own_row = jnp.zeros((1, D_ATTN), F32)
    l_row = jnp.ones((1, D_ATTN), F32)
    for h in range(ATTN_HEADS):
        qc = q_col[h * ATTN_HEAD_DIM:(h + 1) * ATTN_HEAD_DIM]
        base = h * per_head
        s_pages = [jnp.sum(kbuf[base + g] * qc, axis=0, keepdims=True) for g in range(per_head)]
        s_own = jnp.sum(jnp.where(head_of_lane == h, qk_own, 0.0), axis=-1, keepdims=True)
        m = s_own
        for s in s_pages:
            m = jnp.maximum(m, jnp.max(s, axis=-1, keepdims=True))
        p_own = jnp.exp(s_own - m)
        l = p_own
        acc = jnp.zeros(kbuf.shape[1:], F32)
        for g, s in enumerate(s_pages):
            p = jnp.exp(s - m)
            l = l + jnp.sum(p, axis=-1, keepdims=True)
            acc = acc + vbuf[base + g] * p
        acc_cols.append(jnp.sum(acc, axis=-1, keepdims=True))
        p_own_row = jnp.where(head_of_lane == h, p_own, p_own_row)
        l_row = jnp.where(head_of_lane == h, l, l_row)
    acc_col = jnp.concatenate(acc_cols, axis=0)
    acc_row = jnp.concatenate([_col_to_row(acc_col[t * LANES:(t + 1) * LANES]) for t in range(tiles)], axis=1)
    o_ref[0] = (acc_row + p_own_row * v_row) / l_row


def _sample_attend(q, k_new, v_new, kp5, vp5, pt_flat, top_flat, layer, n_pages, n_sel, ppb):
    db = q.shape[0]
    ps = kp5.shape[-1]
    row = pl.BlockSpec((1, 1, D_ATTN), lambda d, pt, top: (d, 0, 0))
    n_buf = ATTN_HEADS * n_sel * ppb
    out = pl.pallas_call(
        functools.partial(_sample_attend_body, layer=layer, n_pages=n_pages, n_sel=n_sel, ppb=ppb),
        grid_spec=pltpu.PrefetchScalarGridSpec(
            num_scalar_prefetch=2,
            grid=(db,),
            in_specs=[row, row, row, pl.BlockSpec(memory_space=pl.ANY), pl.BlockSpec(memory_space=pl.ANY)],
            out_specs=row,
            scratch_shapes=[pltpu.VMEM((n_buf, ATTN_HEAD_DIM, ps), F32), pltpu.VMEM((n_buf, ATTN_HEAD_DIM, ps), F32),
                            pltpu.SemaphoreType.DMA((2,))],
        ),
        out_shape=jax.ShapeDtypeStruct((db, 1, D_ATTN), F32),
        compiler_params=_cparams(("arbitrary",)),
        name="sample_attend",
    )(pt_flat, top_flat, q.reshape(db, 1, D_ATTN), k_new.reshape(db, 1, D_ATTN), v_new.reshape(db, 1, D_ATTN),
      kp5, vp5)
    return out.reshape(db, D_ATTN)


def _moba_sample(q, k_new, v_new, k_pool, v_pool, page_table, layer):
    db = q.shape[0]
    ps = k_pool.shape[2]
    n_pages = page_table.shape[1]
    ppb = MOBA_BLOCK // ps
    nbp = (n_pages * ps) // MOBA_BLOCK
    assert nbp >= 1 and (n_pages - nbp * ppb) * ps == 0, "the cached rows must fill whole MoBA blocks"
    n_sel = min(MOBA_TOPK, nbp)
    kp5 = jnp.transpose(k_pool, (0, 1, 3, 4, 2))
    vp5 = jnp.transpose(v_pool, (0, 1, 3, 4, 2))
    pt_flat = page_table.reshape(-1)
    ksum = _page_sums(kp5, pt_flat, layer, db, n_pages, ppb)
    top = _sample_select(q, ksum, nbp)
    top_flat = top[:, :, :n_sel].reshape(-1)
    return _sample_attend(q, k_new, v_new, kp5, vp5, pt_flat, top_flat, layer, n_pages, n_sel, ppb)


def _mlstm_step_body(qk_ref, vm_ref, if_ref, conv_ref, cw_ref, c_ref, n_ref, m_ref,
                     h_ref, c_out, n_out, m_out):
    tb = qk_ref.shape[0]
    w = cw_ref[...]
    lane = _iota((1, LANES), 1)

    def one(r):
        x = qk_ref[pl.ds(r, 1), :]
        cb = conv_ref[pl.ds(r, 1)][0]
        y = cb[0:1] * w[0:1]
        for j in range(1, CONV_W - 1):
            y = y + cb[j:j + 1] * w[j:j + 1]
        y = y + x * w[CONV_W - 1:CONV_W]
        qkc = _silu(y)
        gates = if_ref[pl.ds(r, 1), :]
        m_all = m_ref[pl.ds(r, 1), :]
        v_all = vm_ref[pl.ds(r, 1), :]
        for h in range(MLSTM_HEADS):
            sl = slice(h * MLSTM_HEAD_DIM, (h + 1) * MLSTM_HEAD_DIM)
            q = qkc[:, sl]
            k = qkc[:, D_MLSTM + h * MLSTM_HEAD_DIM:D_MLSTM + (h + 1) * MLSTM_HEAD_DIM] * (MLSTM_HEAD_DIM ** -0.5)
            v = v_all[:, sl]
            ig = gates[:, h:h + 1]
            fg = gates[:, MLSTM_HEADS + h:MLSTM_HEADS + h + 1]
            m_prev = m_all[:, h:h + 1]
            inter = _log_sigmoid(fg) + m_prev
            m_t = jnp.maximum(inter, ig)
            wgt = jnp.exp(ig - m_t)
            w0 = jnp.exp(inter - m_t)
            c_prev = c_ref[pl.ds(r, 1), h][0]
            n_prev = n_ref[pl.ds(r, 1), h]
            sqk = jnp.sum(q * k, axis=-1, keepdims=True) * wgt
            cq = _nt(jnp.broadcast_to(q, (SUBLANES, MLSTM_HEAD_DIM)), c_prev, precision=HI)[0:1]
            num = w0 * cq + sqk * v
            den = w0 * jnp.sum(n_prev * q, axis=-1, keepdims=True) + sqk
            h_ref[pl.ds(r, 1), sl] = num / jnp.maximum(jnp.abs(den), jnp.exp(-m_t))
            c_out[pl.ds(r, 1), h] = (w0 * c_prev + (wgt * _row_to_col(v)) * k)[None]
            n_out[pl.ds(r, 1), h] = w0 * n_prev + wgt * k
            m_all = jnp.where(lane == h, m_t, m_all)
        m_out[pl.ds(r, 1), :] = m_all

    for r in range(tb):
        one(r)


def _mlstm_step(qk_m, v_m, if_m, conv_state, conv_w, c, n, m_pad, tb):
    db = qk_m.shape[0]
    row = lambda w: pl.BlockSpec((tb, w), lambda i: (i, 0))
    c_spec = pl.BlockSpec((tb, MLSTM_HEADS, MLSTM_HEAD_DIM, MLSTM_HEAD_DIM), lambda i: (i, 0, 0, 0))
    n_spec = pl.BlockSpec((tb, MLSTM_HEADS, MLSTM_HEAD_DIM), lambda i: (i, 0, 0))
    return pl.pallas_call(
        _mlstm_step_body,
        grid=(db // tb,),
        in_specs=[row(2 * D_MLSTM), row(D_MLSTM), row(LANES),
                  pl.BlockSpec((tb, CONV_W - 1, 2 * D_MLSTM), lambda i: (i, 0, 0)),
                  pl.BlockSpec((CONV_W, 2 * D_MLSTM), lambda i: (0, 0)), c_spec, n_spec, row(LANES)],
        out_specs=[row(D_MLSTM), c_spec, n_spec, row(LANES)],
        out_shape=[jax.ShapeDtypeStruct((db, D_MLSTM), F32), jax.ShapeDtypeStruct(c.shape, F32),
                   jax.ShapeDtypeStruct(n.shape, F32), jax.ShapeDtypeStruct((db, LANES), F32)],
        compiler_params=_cparams(("arbitrary",)),
        name="mlstm_step",
    )(qk_m, v_m, if_m, conv_state, conv_w, c, n, m_pad)


def _sample_path(x, mod6, p, k_pool, v_pool, page_table, c0, n0, m0, conv0, layer):
    db, t, d = x.shape
    assert t == 1, "one new token per sequence"
    assert db % SUBLANES == 0
    xs = x.reshape(1, db, d)
    sh1, sc1, g1, sh2, sc2, g2 = [m.reshape(1, db, d) for m in mod6]
    q, k, v, _, _, _, qk_m, v_m, o_m, if_m, _ = _front(xs, sc1, sh1, p["norm1_g"], p["w_in"], p["b_if"], db)
    q, k, v, qk_m, v_m, if_m = [a[0] for a in (q, k, v, qk_m, v_m, if_m)]
    attn = _moba_sample(q, k, v, k_pool, v_pool, page_table, layer)
    m_pad = jnp.pad(m0, ((0, 0), (0, LANES - MLSTM_HEADS)))
    h_s, c_s, n_s, m_s = _mlstm_step(qk_m, v_m, if_m, conv0, p["conv_w"], c0, n0, m_pad, SUBLANES)
    y = _layer_tail(xs, (g1, sh2, sc2, g2), attn[None], h_s[None], o_m, p["attn_norm_g"], p["mlstm_norm_g"],
                    p["w_out"], p["norm2_g"], p["peer_w_query"], p["peer_sub_keys"], p["uv"], p["normf_g"],
                    db, min(64, db))
    conv_s = jnp.concatenate([conv0[:, 1:], qk_m[:, None, :]], axis=1)
    return y.reshape(db, t, d), k, v, c_s, n_s, m_s[:, :MLSTM_HEADS], conv_s


def kernel(x_prompt, x_sample, cache_k_pool, cache_v_pool, page_table, state_mlstm_C, state_mlstm_n, state_mlstm_m, state_conv, c_prompt, c_sample, w_ada, b_ada, norm1_g, w_in, b_if, conv_w, attn_norm_g, mlstm_norm_g, w_out, norm2_g, peer_w_query, peer_sub_keys, peer_u, peer_v, normf_g):
    b, s, d = x_prompt.shape
    db = x_sample.shape[0]
    l = 0
    p = dict(norm1_g=norm1_g[l], w_in=w_in[l], b_if=b_if[l], conv_w=conv_w[l], attn_norm_g=attn_norm_g[l],
             mlstm_norm_g=mlstm_norm_g[l], w_out=w_out[l], norm2_g=norm2_g[l], peer_w_query=peer_w_query[l],
             peer_sub_keys=peer_sub_keys[l], normf_g=normf_g,
             uv=jnp.concatenate([_pack_bf16_pairs(peer_u[l]), _pack_bf16_pairs(peer_v[l])], axis=1)[:, None, :])
    rows = b + db
    pad = -rows % SUBLANES
    c_all = jnp.pad(jnp.concatenate([c_prompt, c_sample], axis=0), ((0, pad), (0, 0)))
    mod = _ada(c_all, w_ada[l], b_ada[l])
    mod_p = [mod[:b, i * d:(i + 1) * d] for i in range(6)]
    mod_s = [mod[b:rows, i * d:(i + 1) * d] for i in range(6)]
    y_p, k_p, v_p, c_p, n_p, m_p, conv_p = _prompt_path(x_prompt, mod_p, p)
    y_s, k_s, v_s, c_s, n_s, m_s, conv_s = _sample_path(
        x_sample, mod_s, p, cache_k_pool, cache_v_pool, page_table,
        state_mlstm_C[l], state_mlstm_n[l], state_mlstm_m[l], state_conv[l], l)
    hp = (b, s, ATTN_HEADS, ATTN_HEAD_DIM)
    hs = (db, 1, ATTN_HEADS, ATTN_HEAD_DIM)
    return (y_p, y_s, k_p.reshape(hp)[None], v_p.reshape(hp)[None], k_s.reshape(hs)[None], v_s.reshape(hs)[None],
            c_p[None], n_p[None], m_p[None], conv_p[None], c_s[None], n_s[None], m_s[None], conv_s[None])
```

```python
import functools

import jax
import jax.numpy as jnp
from jax import lax
from jax.experimental import pallas as pl
from jax.experimental.pallas import tpu as pltpu

F32 = jnp.float32
BF16 = jnp.bfloat16
I32 = jnp.int32
HI = lax.Precision.HIGHEST

SUBLANES = 8
LANES = 128
VMEM_LIMIT_BYTES = 56 * 1024 * 1024

D_MODEL = 1024
D_ATTN = 512
D_MLSTM = 512
ATTN_HEADS = 8
ATTN_HEAD_DIM = 64
MLSTM_HEADS = 4
MLSTM_HEAD_DIM = 128
MOBA_BLOCK = 256
MOBA_TOPK = 3
CONV_W = 4
PEER_HEADS = 8
PEER_N_KEYS = 128
PEER_TOPK = 16
PEER_HALF = 128
EPS = 1e-6
N_GATES = 2 * MLSTM_HEADS
C_Q, C_K, C_V, C_QK, C_VM, C_OM, C_IF = 0, 512, 1024, 1536, 2560, 3072, 3584
NEG = -1e30


def _cparams(sem):
    return pltpu.CompilerParams(dimension_semantics=sem, vmem_limit_bytes=VMEM_LIMIT_BYTES)


def _nt(a, b, **kw):
    return lax.dot_general(a, b, (((1,), (1,)), ((), ())), preferred_element_type=F32, **kw)


def _tn(a, b, **kw):
    return lax.dot_general(a, b, (((0,), (0,)), ((), ())), preferred_element_type=F32, **kw)


def _silu(x):
    return x * jax.nn.sigmoid(x)


def _log_sigmoid(x):
    return jnp.minimum(x, 0.0) - jnp.log1p(jnp.exp(-jnp.abs(x)))


def _iota(shape, dim):
    return lax.broadcasted_iota(I32, shape, dim)


def _ada_body(c_ref, w_ref, b_ref, o_ref):
    o_ref[...] = jnp.dot(_silu(c_ref[...]), w_ref[...], precision=HI, preferred_element_type=F32) + b_ref[...]


def _ada(c, w_ada, b_ada):
    rows, d = c.shape
    n = w_ada.shape[1]
    tn = 512
    return pl.pallas_call(
        _ada_body,
        grid=(n // tn,),
        in_specs=[pl.BlockSpec((rows, d), lambda j: (0, 0)),
                  pl.BlockSpec((d, tn), lambda j: (0, j)),
                  pl.BlockSpec((1, tn), lambda j: (0, j))],
        out_specs=pl.BlockSpec((rows, tn), lambda j: (0, j)),
        out_shape=jax.ShapeDtypeStruct((rows, n), F32),
        compiler_params=_cparams(("arbitrary",)),
        name="ada_mod",
    )(c, w_ada, b_ada.reshape(1, n))


def _front_body(x_ref, sc_ref, sh_ref, g_ref, w_ref, wif_ref, wift_ref, bif_ref, bift_ref,
                q_ref, k_ref, v_ref, qb_ref, kb_ref, vb_ref, qk_ref, vm_ref, om_ref, if_ref, ift_ref):
    x = x_ref[0]
    h = x * lax.rsqrt(jnp.mean(x * x, axis=-1, keepdims=True) + EPS) * g_ref[...]
    h = h * (1.0 + sc_ref[0]) + sh_ref[0]
    hb = h.astype(BF16)

    def seg(lo, hi):
        return jnp.dot(hb, w_ref[:, lo:hi], preferred_element_type=F32)

    q = seg(C_Q, C_K)
    q_ref[0] = q
    qb_ref[0] = (q * (ATTN_HEAD_DIM ** -0.5)).astype(BF16)
    k = seg(C_K, C_V)
    k_ref[0] = k
    kb_ref[0] = k.astype(BF16)
    v = seg(C_V, C_QK)
    v_ref[0] = v
    vb_ref[0] = v.astype(BF16)
    qk_ref[0] = seg(C_QK, C_VM)
    vm_ref[0] = seg(C_VM, C_OM)
    om_ref[0] = seg(C_OM, C_IF)
    if_ref[0] = jnp.dot(h, wif_ref[...], precision=HI, preferred_element_type=F32) + bif_ref[...]
    ift_ref[0] = _nt(wift_ref[...], h, precision=HI) + bift_ref[...]


def _front(x, sc, sh, norm_g, w_in, b_if, tm):
    b, s, d = x.shape
    per_row = sc.shape[1] != 1
    mod_spec = (pl.BlockSpec((1, tm, d), lambda bi, i: (bi, i, 0)) if per_row
                else pl.BlockSpec((1, 1, d), lambda bi, i: (bi, 0, 0)))
    w_main = w_in[:, :C_IF].astype(BF16)
    w_if = jnp.pad(w_in[:, C_IF:], ((0, 0), (0, LANES - N_GATES)))
    w_ift = w_in[:, C_IF:].T
    b_row = jnp.pad(b_if, (0, LANES - N_GATES)).reshape(1, LANES)
    b_col = b_if.reshape(N_GATES, 1)
    const = lambda shape: pl.BlockSpec(shape, lambda bi, i: (0,) * len(shape))
    tok = lambda w: pl.BlockSpec((1, tm, w), lambda bi, i: (bi, i, 0))
    out_shapes = [
        jax.ShapeDtypeStruct((b, s, D_ATTN), F32),
        jax.ShapeDtypeStruct((b, s, D_ATTN), F32),
        jax.ShapeDtypeStruct((b, s, D_ATTN), F32),
        jax.ShapeDtypeStruct((b, s, D_ATTN), BF16),
        jax.ShapeDtypeStruct((b, s, D_ATTN), BF16),
        jax.ShapeDtypeStruct((b, s, D_ATTN), BF16),
        jax.ShapeDtypeStruct((b, s, 2 * D_MLSTM), F32),
        jax.ShapeDtypeStruct((b, s, D_MLSTM), F32),
        jax.ShapeDtypeStruct((b, s, D_MLSTM), F32),
        jax.ShapeDtypeStruct((b, s, LANES), F32),
        jax.ShapeDtypeStruct((b, N_GATES, s), F32),
    ]
    out_specs = [tok(D_ATTN)] * 6 + [tok(2 * D_MLSTM), tok(D_MLSTM), tok(D_MLSTM), tok(LANES),
                                     pl.BlockSpec((1, N_GATES, tm), lambda bi, i: (bi, 0, i))]
    return pl.pallas_call(
        _front_body,
        grid=(b, s // tm),
        in_specs=[tok(d), mod_spec, mod_spec, const((1, d)), const((d, C_IF)), const((d, LANES)),
                  const((N_GATES, d)), const((1, LANES)), const((N_GATES, 1))],
        out_specs=out_specs,
        out_shape=out_shapes,
        compiler_params=_cparams(("arbitrary", "arbitrary")),
        name="front",
    )(x, sc, sh, norm_g.reshape(1, d), w_main, w_if, w_ift, b_row, b_col)


def _km_body(k_ref, km_ref, *, hb):
    n = pl.program_id(1)

    @pl.when(n == 0)
    def _():
        km_ref[...] = jnp.zeros_like(km_ref)

    lw = km_ref.shape[2]
    ones = jnp.full((MOBA_BLOCK, lw), 1.0 / MOBA_BLOCK, F32)
    mean_b = _tn(k_ref[0], ones, precision=HI)
    row = _iota((D_ATTN, lw), 0)
    col = _iota((D_ATTN, lw), 1)
    km_ref[0] += jnp.where(col == (row // ATTN_HEAD_DIM) * hb + n, mean_b, 0.0)


def _block_means(k, nb, hb):
    lw = ATTN_HEADS * hb
    b = k.shape[0]
    return pl.pallas_call(
        functools.partial(_km_body, hb=hb),
        grid=(b, nb),
        in_specs=[pl.BlockSpec((1, MOBA_BLOCK, D_ATTN), lambda bi, n: (bi, n, 0))],
        out_specs=pl.BlockSpec((1, D_ATTN, lw), lambda bi, n: (bi, 0, 0)),
        out_shape=jax.ShapeDtypeStruct((b, D_ATTN, lw), F32),
        compiler_params=_cparams(("arbitrary", "arbitrary")),
        name="moba_block_means",
    )(k)


def _sel_body(q_ref, km_ref, o_ref, *, hb, n_top):
    own = pl.program_id(1)
    s = jnp.dot(q_ref[0], km_ref[0], precision=HI, preferred_element_type=F32)
    shape = s.shape
    lw = shape[1]
    lane = _iota(shape, 1)
    blk = lane & (hb - 1)
    head = lane >> (hb.bit_length() - 1)
    valid = (blk < own) & (head < ATTN_HEADS)
    sel = jnp.zeros(shape, jnp.bool_)
    for h in range(ATTN_HEADS):
        sm = jnp.where(valid & (head == h), s, -jnp.inf)
        for _ in range(n_top):
            m = jnp.max(sm, axis=-1, keepdims=True)
            i = jnp.min(jnp.where(sm == m, lane, lw), axis=-1, keepdims=True)
            hit = lane == i
            sel = sel | (hit & (m > -jnp.inf))
            sm = jnp.where(hit, -jnp.inf, sm)
    o_ref[0] = jnp.where(sel, 0.0, NEG)


def _moba_select(q, km, nb, hb):
    b, s, _ = q.shape
    lw = km.shape[2]
    return pl.pallas_call(
        functools.partial(_sel_body, hb=hb, n_top=min(MOBA_TOPK, nb)),
        grid=(b, s // MOBA_BLOCK),
        in_specs=[pl.BlockSpec((1, MOBA_BLOCK, D_ATTN), lambda bi, i: (bi, i, 0)),
                  pl.BlockSpec((1, D_ATTN, lw), lambda bi, i: (bi, 0, 0))],
        out_specs=pl.BlockSpec((1, MOBA_BLOCK, lw), lambda bi, i: (bi, i, 0)),
        out_shape=jax.ShapeDtypeStruct((b, s, lw), F32),
        compiler_params=_cparams(("arbitrary", "arbitrary")),
        name="moba_select",
    )(q, km)


def _attn_body(q_ref, k_ref, v_ref, sb_ref, lid_ref, o_ref, *, hb):
    own = pl.program_id(2)
    tq = MOBA_BLOCK
    tiles = q_ref.shape[2] // LANES
    n_heads = 2 * tiles
    lane = _iota((tq, LANES), 1)
    low = lane < ATTN_HEAD_DIM
    sb = sb_ref[0]
    one = jnp.ones((tq, LANES), BF16)
    zero = jnp.zeros((tq, LANES), BF16)
    lane_b = lid_ref[...]
    low_b = lane_b < ATTN_HEAD_DIM
    qs, mine, bias0 = [], [], []
    for hh in range(n_heads):
        t = hh // 2
        q = q_ref[0, :, t * LANES:(t + 1) * LANES]
        is_low = hh % 2 == 0
        first = ATTN_HEAD_DIM if is_low else 0
        tile, off = divmod(hh * hb, LANES)
        bias = sb[:, tile * LANES:(tile + 1) * LANES]
        shift = (first - off) % LANES
        if shift:
            bias = pltpu.roll(bias, shift, 1)
        keep = low_b if is_low else jnp.logical_not(low_b)
        qs.append(jnp.where(keep, q, bias.astype(BF16)))
        mine.append(keep)
        bias0.append(first)
    row = _iota((tq, tq), 0)
    col = _iota((tq, tq), 1)

    def attend(j, state, is_own):
        start = pl.multiple_of(j * tq, tq)
        new = []
        for hh in range(n_heads):
            sl = slice((hh // 2) * LANES, (hh // 2 + 1) * LANES)
            kj = k_ref[0, pl.ds(start, tq), sl]
            vj = v_ref[0, pl.ds(start, tq), sl]
            m, acc = state[hh]
            if is_own:
                s = _nt(qs[hh], jnp.where(mine[hh], kj, zero))
                s = jnp.where(col <= row, s, -jnp.inf)
            else:
                hot = lane_b == (bias0[hh] + j).astype(F32).astype(BF16)
                s = _nt(qs[hh], jnp.where(mine[hh], kj, jnp.where(hot, one, zero)))
            m_new = jnp.maximum(m, jnp.max(s, axis=-1, keepdims=True))
            p = jnp.exp(s - m_new)
            acc = jnp.exp(m - m_new) * acc + jnp.dot(p.astype(BF16), jnp.where(mine[hh], vj, one),
                                                      preferred_element_type=F32)
            new.append((m_new, acc))
        return tuple(new)

    init = tuple((jnp.full((tq, 1), -jnp.inf, F32), jnp.zeros((tq, LANES), F32)) for _ in range(n_heads))
    state = attend(own, init, True)
    state = lax.fori_loop(0, own, lambda j, st: attend(j, st, False), state)
    for t in range(tiles):
        outs = []
        for hh in (2 * t, 2 * t + 1):
            acc = state[hh][1]
            outs.append(acc / pltpu.roll(acc, ATTN_HEAD_DIM, 1))
        o_ref[0, :, t * LANES:(t + 1) * LANES] = jnp.where(low, outs[0], outs[1])


ATTN_LANES_PER_STEP = 2 * LANES


def _moba_attend(qb, kb, vb, selbias, hb):
    b, s, _ = qb.shape
    gw = ATTN_LANES_PER_STEP
    groups = D_ATTN // gw
    sw = selbias.shape[2] // groups
    assert sw % LANES == 0
    return pl.pallas_call(
        functools.partial(_attn_body, hb=hb),
        grid=(b, groups, s // MOBA_BLOCK),
        in_specs=[pl.BlockSpec((1, MOBA_BLOCK, gw), lambda bi, p, i: (bi, i, p)),
                  pl.BlockSpec((1, s, gw), lambda bi, p, i: (bi, 0, p)),
                  pl.BlockSpec((1, s, gw), lambda bi, p, i: (bi, 0, p)),
                  pl.BlockSpec((1, MOBA_BLOCK, sw), lambda bi, p, i: (bi, i, p)),
                  pl.BlockSpec((MOBA_BLOCK, LANES), lambda bi, p, i: (0, 0))],
        out_specs=pl.BlockSpec((1, MOBA_BLOCK, gw), lambda bi, p, i: (bi, i, p)),
        out_shape=jax.ShapeDtypeStruct((b, s, D_ATTN), F32),
        compiler_params=_cparams(("arbitrary", "arbitrary", "arbitrary")),
        name="moba_attend",
    )(qb, kb, vb, selbias, jnp.broadcast_to(jnp.arange(LANES, dtype=F32).astype(BF16), (MOBA_BLOCK, LANES)))


MIN_HEAD_STRIDE = 32


def _moba_prompt(q, k, qb, kb, vb):
    s = q.shape[1]
    assert s % MOBA_BLOCK == 0
    nb = s // MOBA_BLOCK
    assert nb & (nb - 1) == 0, "block count must be a power of two"
    hb = max(nb, MIN_HEAD_STRIDE)
    assert hb <= ATTN_HEAD_DIM, "the per-block bias must fit in the free half of a lane tile"
    km = _block_means(k, nb, hb)
    selbias = _moba_select(q, km, nb, hb)
    return _moba_attend(qb, kb, vb, selbias, hb)


def _mlstm_body(qk_ref, vm_ref, if_ref, ift_ref, cw_ref, h_ref, c_ref, n_ref, m_ref, prev_scr):
    i = pl.program_id(1)
    L = qk_ref.shape[1]

    @pl.when(i == 0)
    def _():
        prev_scr[...] = jnp.zeros_like(prev_scr)
        c_ref[...] = jnp.zeros_like(c_ref)
        n_ref[...] = jnp.zeros_like(n_ref)
        m_ref[...] = jnp.zeros_like(m_ref)

    x = qk_ref[0]
    prev = prev_scr[...]
    w = cw_ref[...]
    rows = _iota(x.shape, 0)
    y = x * w[CONV_W - 1:CONV_W]
    for sft in range(1, CONV_W):
        xs = jnp.where(rows >= sft, pltpu.roll(x, sft, 0), pltpu.roll(prev, sft, 0))
        y = y + xs * w[CONV_W - 1 - sft:CONV_W - sft]
    prev_scr[...] = x
    qkc = _silu(y)

    gcol = if_ref[0]
    grow = ift_ref[0]
    tri_r = _iota((L, L), 0)
    tri_c = _iota((L, L), 1)
    lower = tri_c <= tri_r
    b_col = jnp.dot(lower.astype(F32), _log_sigmoid(gcol), precision=HI, preferred_element_type=F32)
    b_row = jnp.dot(_log_sigmoid(grow), (tri_r <= tri_c).astype(F32), precision=HI, preferred_element_type=F32)
    m_all = m_ref[0]
    lane = _iota(m_all.shape, 1)
    for h in range(MLSTM_HEADS):
        sl = slice(h * MLSTM_HEAD_DIM, (h + 1) * MLSTM_HEAD_DIM)
        qh = qkc[:, sl]
        kh = qkc[:, D_MLSTM + h * MLSTM_HEAD_DIM:D_MLSTM + (h + 1) * MLSTM_HEAD_DIM] * (MLSTM_HEAD_DIM ** -0.5)
        vh = vm_ref[0][:, sl]
        fh = MLSTM_HEADS + h
        bc = b_col[:, fh:fh + 1]
        br = b_row[fh:fh + 1, :]
        ic = gcol[:, h:h + 1]
        ir = grow[h:h + 1, :]
        m_prev = m_all[:, h:h + 1]
        c_prev = c_ref[0, h]
        n_prev = n_ref[0, h:h + 1, :]
        dmat = jnp.where(lower, bc - br + ir, -jnp.inf)
        inter = bc + m_prev
        m_t = jnp.maximum(inter, jnp.max(dmat, axis=-1, keepdims=True))
        wmat = jnp.exp(dmat - m_t)
        w0 = jnp.exp(inter - m_t)
        qb, kb, vb = qh.astype(BF16), kh.astype(BF16), vh.astype(BF16)
        sqk = _nt(qb, kb) * wmat
        num = w0 * _nt(qb, c_prev.astype(BF16)) + jnp.dot(sqk.astype(BF16), vb, preferred_element_type=F32)
        den = w0 * jnp.sum(qh * n_prev, axis=-1, keepdims=True) + jnp.sum(sqk, axis=-1, keepdims=True)
        h_ref[0, :, sl] = num / jnp.maximum(jnp.abs(den), jnp.exp(-m_t))
        m_new = m_t[L - 1:L, :]
        b_last = bc[L - 1:L, :]
        g = jnp.exp(b_last - bc + ic - m_new)
        decay = jnp.exp(b_last + m_prev - m_new)
        c_ref[0, h] = decay * c_prev + _tn((g * vh).astype(BF16), kb)
        n_ref[0, h:h + 1, :] = decay * n_prev + jnp.sum(g * kh, axis=0, keepdims=True)
        m_all = jnp.where(lane == h, m_new, m_all)
    m_ref[0] = m_all


def _mlstm_prompt(qk_m, v_m, if_m, if_t, conv_w, chunk):
    b, s, _ = qk_m.shape
    tok = lambda w: pl.BlockSpec((1, chunk, w), lambda bi, i: (bi, i, 0))
    return pl.pallas_call(
        _mlstm_body,
        grid=(b, s // chunk),
        in_specs=[tok(2 * D_MLSTM), tok(D_MLSTM), tok(LANES),
                  pl.BlockSpec((1, N_GATES, chunk), lambda bi, i: (bi, 0, i)),
                  pl.BlockSpec((CONV_W, 2 * D_MLSTM), lambda bi, i: (0, 0))],
        out_specs=[tok(D_MLSTM),
                   pl.BlockSpec((1, MLSTM_HEADS, MLSTM_HEAD_DIM, MLSTM_HEAD_DIM), lambda bi, i: (bi, 0, 0, 0)),
                   pl.BlockSpec((1, MLSTM_HEADS, MLSTM_HEAD_DIM), lambda bi, i: (bi, 0, 0)),
                   pl.BlockSpec((1, 1, LANES), lambda bi, i: (bi, 0, 0))],
        out_shape=[jax.ShapeDtypeStruct((b, s, D_MLSTM), F32),
                   jax.ShapeDtypeStruct((b, MLSTM_HEADS, MLSTM_HEAD_DIM, MLSTM_HEAD_DIM), F32),
                   jax.ShapeDtypeStruct((b, MLSTM_HEADS, MLSTM_HEAD_DIM), F32),
                   jax.ShapeDtypeStruct((b, 1, LANES), F32)],
        scratch_shapes=[pltpu.VMEM((chunk, 2 * D_MLSTM), F32)],
        compiler_params=_cparams(("arbitrary", "arbitrary")),
        name="mlstm_prompt",
    )(qk_m, v_m, if_m, if_t, conv_w)


def _back_body(x_ref, a_ref, hm_ref, om_ref, g1_ref, sc_ref, sh_ref, ag_ref, mg_ref, wo_ref, n2_ref, wq_ref,
               x1_ref, h2_ref, pq_ref):
    a = a_ref[0]
    tm = a.shape[0]
    lane = _iota((tm, LANES), 1)
    low = lane < ATTN_HEAD_DIM
    proj = jnp.zeros((tm, D_MODEL), F32)
    for t in range(D_ATTN // LANES):
        sl = slice(t * LANES, (t + 1) * LANES)
        at = a[:, sl]
        sq = at * at
        s_lo = jnp.sum(jnp.where(low, sq, 0.0), axis=-1, keepdims=True)
        s_hi = jnp.sum(jnp.where(low, 0.0, sq), axis=-1, keepdims=True)
        scale = jnp.where(low, lax.rsqrt(s_lo / ATTN_HEAD_DIM + EPS), lax.rsqrt(s_hi / ATTN_HEAD_DIM + EPS))
        an = at * scale * ag_ref[:, sl]
        proj = proj + jnp.dot(an.astype(BF16), wo_ref[sl, :], preferred_element_type=F32)
    hm = hm_ref[0]
    om = om_ref[0]
    for t in range(MLSTM_HEADS):
        sl = slice(t * LANES, (t + 1) * LANES)
        ht = hm[:, sl]
        hn = ht * lax.rsqrt(jnp.mean(ht * ht, axis=-1, keepdims=True) + EPS) * mg_ref[:, sl]
        mm = jax.nn.sigmoid(om[:, sl]) * hn
        proj = proj + jnp.dot(mm.astype(BF16), wo_ref[D_ATTN + t * LANES:D_ATTN + (t + 1) * LANES, :],
                              preferred_element_type=F32)
    x1 = x_ref[0] + g1_ref[0] * proj
    x1_ref[0] = x1
    h2 = x1 * lax.rsqrt(jnp.mean(x1 * x1, axis=-1, keepdims=True) + EPS) * n2_ref[...]
    h2 = h2 * (1.0 + sc_ref[0]) + sh_ref[0]
    h2_ref[0] = h2
    pq_ref[0] = jnp.dot(h2.astype(BF16), wq_ref[...], preferred_element_type=F32)


def _back(x, attn, h_m, o_m, g1, sc2, sh2, attn_g, mlstm_g, w_out, norm2_g, w_query, tm):
    b, s, d = x.shape
    per_row = g1.shape[1] != 1
    mod_spec = (pl.BlockSpec((1, tm, d), lambda bi, i: (bi, i, 0)) if per_row
                else pl.BlockSpec((1, 1, d), lambda bi, i: (bi, 0, 0)))
    nq = w_query.shape[1]
    const = lambda shape: pl.BlockSpec(shape, lambda bi, i: (0,) * len(shape))
    tok = lambda w: pl.BlockSpec((1, tm, w), lambda bi, i: (bi, i, 0))
    return pl.pallas_call(
        _back_body,
        grid=(b, s // tm),
        in_specs=[tok(d), tok(D_ATTN), tok(D_MLSTM), tok(D_MLSTM), mod_spec, mod_spec, mod_spec,
                  const((1, D_ATTN)), const((1, D_MLSTM)), const((d, d)), const((1, d)), const((d, nq))],
        out_specs=[tok(d), tok(d), tok(nq)],
        out_shape=[jax.ShapeDtypeStruct((b, s, d), F32), jax.ShapeDtypeStruct((b, s, d), F32),
                   jax.ShapeDtypeStruct((b, s, nq), F32)],
        compiler_params=_cparams(("arbitrary", "arbitrary")),
        name="back",
    )(x, attn, h_m, o_m, g1, sc2, sh2, attn_g.reshape(1, D_ATTN), mlstm_g.reshape(1, D_MLSTM),
      w_out.astype(BF16), norm2_g.reshape(1, d), w_query.astype(BF16))


def _peer_route_body(pq_ref, sk_ref, idx_ref, gate_ref):
    tp = pq_ref.shape[0]
    k = PEER_TOPK
    n_exp = PEER_N_KEYS * PEER_N_KEYS
    out_row = _iota((k, tp), 0)

    def extract(score, tag, fill):
        vals = jnp.zeros((k, tp), F32)
        tags = jnp.zeros((k, tp), F32)
        for r in range(k):
            m = jnp.max(score, axis=0, keepdims=True)
            t = jnp.min(jnp.where(score == m, tag, fill), axis=0, keepdims=True)
            vals = jnp.where(out_row == r, m, vals)
            tags = jnp.where(out_row == r, t, tags)
            score = jnp.where(tag == t, -jnp.inf, score)
        return vals, tags

    key_row = _iota((PEER_N_KEYS, tp), 0).astype(F32)
    pos = _iota((k * k, tp), 0).astype(F32)
    gates, ids = [], []
    for h in range(PEER_HEADS):
        halves = []
        for p in range(2):
            c0 = (2 * h + p) * PEER_HALF
            qh = pq_ref[:, c0:c0 + PEER_HALF].astype(BF16)
            s = _nt(sk_ref[2 * h + p].astype(BF16), qh)
            halves.append(extract(s, key_row, float(PEER_N_KEYS)))
        (v0, i0), (v1, i1) = halves
        cand = jnp.concatenate([v0[a:a + 1] + v1 for a in range(k)], axis=0)
        eid = jnp.concatenate([i0[a:a + 1] * PEER_N_KEYS + i1 for a in range(k)], axis=0)
        comb = pos * n_exp + eid
        c_s, c_key = extract(cand, comb, float(k * k * n_exp))
        e = jnp.exp(c_s - c_s[0:1])
        gates.append(e / jnp.sum(e, axis=0, keepdims=True))
        ids.append(c_key)
    gate_ref[...] = jnp.concatenate(gates, axis=0).T
    keys = jnp.concatenate(ids, axis=0).T.astype(I32)
    idx_ref[...] = keys & (n_exp - 1)


def _peer_route(pq, sub_keys):
    n = pq.shape[0]
    tp = LANES
    n_pad = -(-n // tp) * tp
    pq = jnp.pad(pq, ((0, n_pad - n), (0, 0)))
    sk = sub_keys.reshape(PEER_HEADS * 2, PEER_N_KEYS, PEER_HALF)
    idx, gate = pl.pallas_call(
        _peer_route_body,
        grid=(n_pad // tp,),
        in_specs=[pl.BlockSpec((tp, pq.shape[1]), lambda i: (i, 0)),
                  pl.BlockSpec(sk.shape, lambda i: (0, 0, 0))],
        out_specs=[pl.BlockSpec((tp, LANES), lambda i: (i, 0)), pl.BlockSpec((tp, LANES), lambda i: (i, 0))],
        out_shape=[jax.ShapeDtypeStruct((n_pad, LANES), I32), jax.ShapeDtypeStruct((n_pad, LANES), F32)],
        compiler_params=_cparams(("arbitrary",)),
        name="peer_route",
    )(pq, sk)
    return idx[:n], gate[:n]


N_SEL = PEER_HEADS * PEER_TOPK
EXPERT_SLOTS = 8


def _pack_bf16_pairs(a):
    half = a.shape[1] // 2
    b = lax.bitcast_convert_type(a.astype(BF16), jnp.uint16).astype(jnp.uint32)
    return (b[:, :half] << 16) | b[:, half:]


def _peer_expert_body(idx_ref, gate_ref, h2_ref, x1_ref, g2_ref, nf_ref, uv_ref, y_ref, *scratch):
    n_slots = EXPERT_SLOTS
    bufs = scratch[:n_slots]
    sem, peer_scr = scratch[n_slots], scratch[n_slots + 1]
    tt = h2_ref.shape[1]
    half = D_MODEL // 2

    def issue(t, slot):
        for k in range(N_SEL):
            e = idx_ref[0, 0, t * N_SEL + k]
            pltpu.make_async_copy(uv_ref.at[e], bufs[slot].at[pl.ds(k, 1)],
                                  sem.at[slot]).start(priority=k % 2)

    def wait_all(slot):
        pltpu.make_async_copy(uv_ref.at[pl.ds(0, N_SEL), 0], bufs[slot], sem.at[slot]).wait()

    rr = _iota((N_SEL, LANES), 0)
    cc = _iota((N_SEL, LANES), 1)
    diag = rr == cc

    def unpack(words):
        hi = lax.bitcast_convert_type(words & jnp.uint32(0xFFFF0000), F32)
        lo = lax.bitcast_convert_type(words << 16, F32)
        return hi, lo

    def compute(t, slot):
        x = h2_ref[0, pl.ds(t, 1), :]
        u_hi, u_lo = unpack(bufs[slot][:, 0:half])
        s = jnp.sum(u_hi * x[:, 0:half] + u_lo * x[:, half:D_MODEL], axis=-1, keepdims=True)
        act = 0.5 * s * (1.0 + lax.erf(s * (0.5 ** 0.5)))
        g_row = gate_ref[0, pl.ds(t, 1), :]
        g_col = jnp.sum(jnp.where(diag, jnp.broadcast_to(g_row, (N_SEL, LANES)), 0.0), axis=-1, keepdims=True)
        w = g_col * act
        v_hi, v_lo = unpack(bufs[slot][:, half:2 * half])
        peer_scr[pl.ds(t, 1), 0:half] = jnp.sum(v_hi * w, axis=0, keepdims=True)
        peer_scr[pl.ds(t, 1), half:D_MODEL] = jnp.sum(v_lo * w, axis=0, keepdims=True)

    ahead = n_slots - 2
    for t in range(ahead):
        issue(t, t)

    def token_group(t0, last):
        for sl in range(n_slots):
            if not last or sl + ahead < n_slots:
                issue(t0 + sl + ahead, (sl + ahead) % n_slots)
            wait_all(sl)
            compute(t0 + sl, sl)

    def steady(i, carry):
        token_group(n_slots * i, False)
        return carry

    n_groups = tt // n_slots
    lax.fori_loop(0, n_groups - 1, steady, 0)
    token_group(n_slots * (n_groups - 1), True)
    y = x1_ref[0] + g2_ref[0] * peer_scr[...]
    y_ref[0] = y * lax.rsqrt(jnp.mean(y * y, axis=-1, keepdims=True) + EPS) * nf_ref[...]


def _peer_experts(idx, gate, h2, x1, g2, normf_g, uv, tt):
    b, s, d = h2.shape
    n_slots = EXPERT_SLOTS
    assert tt % n_slots == 0 and tt >= n_slots and s % tt == 0
    nt = s // tt
    per_row = g2.shape[1] != 1
    mod_spec = (pl.BlockSpec((1, tt, d), lambda bi, i: (bi, i, 0)) if per_row
                else pl.BlockSpec((1, 1, d), lambda bi, i: (bi, 0, 0)))
    tok = lambda w: pl.BlockSpec((1, tt, w), lambda bi, i: (bi, i, 0))
    idx3 = idx.reshape(b * nt, 1, tt * N_SEL)
    return pl.pallas_call(
        _peer_expert_body,
        grid=(b, nt),
        in_specs=[pl.BlockSpec((1, 1, tt * N_SEL), lambda bi, i: (bi * nt + i, 0, 0), memory_space=pltpu.SMEM),
                  tok(LANES), tok(d), tok(d), mod_spec,
                  pl.BlockSpec((1, d), lambda bi, i: (0, 0)),
                  pl.BlockSpec(memory_space=pl.ANY)],
        out_specs=tok(d),
        out_shape=jax.ShapeDtypeStruct((b, s, d), F32),
        scratch_shapes=[pltpu.VMEM((N_SEL, uv.shape[2]), jnp.uint32)] * n_slots + [
                        pltpu.SemaphoreType.DMA((n_slots,)),
                        pltpu.VMEM((tt, d), F32)],
        compiler_params=_cparams(("arbitrary", "arbitrary")),
        name="peer_experts",
    )(idx3, gate.reshape(b, s, LANES), h2, x1, g2, normf_g.reshape(1, d), uv)


def _layer_tail(x, mods, attn, h_m, o_m, attn_g, mlstm_g, w_out, norm2_g, w_query, sub_keys, uv, normf_g, tm, tt):
    g1, sh2, sc2, g2 = mods
    b, s, d = x.shape
    x1, h2, pq = _back(x, attn, h_m, o_m, g1, sc2, sh2, attn_g, mlstm_g, w_out, norm2_g, w_query, tm)
    idx, gate = _peer_route(pq.reshape(b * s, -1), sub_keys)
    return _peer_experts(idx, gate, h2, x1, g2, normf_g, uv, tt)


def _prompt_path(x, mod6, p):
    b, s, d = x.shape
    sh1, sc1, g1, sh2, sc2, g2 = [m.reshape(b, 1, d) for m in mod6]
    tm = min(512, s)
    q, k, v, qb, kb, vb, qk_m, v_m, o_m, if_m, if_t = _front(x, sc1, sh1, p["norm1_g"], p["w_in"], p["b_if"], tm)
    attn = _moba_prompt(q, k, qb, kb, vb)
    h_m, c_p, n_p, m_p = _mlstm_prompt(qk_m, v_m, if_m, if_t, p["conv_w"], min(256, s))
    y = _layer_tail(x, (g1, sh2, sc2, g2), attn, h_m, o_m, p["attn_norm_g"], p["mlstm_norm_g"], p["w_out"],
                    p["norm2_g"], p["peer_w_query"], p["peer_sub_keys"], p["uv"], p["normf_g"],
                    min(256, s), min(64, s))
    conv_p = qk_m[:, s - (CONV_W - 1):, :]
    return y, k, v, c_p, n_p, m_p[:, 0, :MLSTM_HEADS], conv_p


PAGES_PER_STEP = 8


def _page_sum_body(pt_ref, *refs, n_in, ppb):
    del pt_ref
    g = pl.program_id(1)
    out_ref = refs[n_in]

    @pl.when(g == 0)
    def _():
        out_ref[...] = jnp.zeros_like(out_ref)

    lane = _iota((D_ATTN, LANES), 1)
    acc = out_ref[0]
    for j in range(0, n_in, ppb):
        pages = refs[j][0, 0]
        for jj in range(1, ppb):
            pages = pages + refs[j + jj][0, 0]
        block_sum = jnp.sum(pages.reshape(D_ATTN, pages.shape[-1]), axis=-1, keepdims=True)
        acc = jnp.where(lane == (g * n_in + j) // ppb, block_sum, acc)
    out_ref[0] = acc


def _page_sums(kp5, pt_flat, layer, db, n_pages, ppb):
    assert n_pages <= LANES
    n_in = min(PAGES_PER_STEP, n_pages)
    assert n_pages % n_in == 0 and n_in % ppb == 0
    ps = kp5.shape[-1]

    def page_spec(j):
        return pl.BlockSpec((1, 1, ATTN_HEADS, ATTN_HEAD_DIM, ps),
                            lambda d, g, pt: (layer, pt[d * n_pages + g * n_in + j], 0, 0, 0))

    return pl.pallas_call(
        functools.partial(_page_sum_body, n_in=n_in, ppb=ppb),
        grid_spec=pltpu.PrefetchScalarGridSpec(
            num_scalar_prefetch=1,
            grid=(db, n_pages // n_in),
            in_specs=[page_spec(j) for j in range(n_in)],
            out_specs=pl.BlockSpec((1, D_ATTN, LANES), lambda d, g, pt: (d, 0, 0)),
        ),
        out_shape=jax.ShapeDtypeStruct((db, D_ATTN, LANES), F32),
        compiler_params=_cparams(("arbitrary", "arbitrary")),
        name="sample_page_sums",
    )(pt_flat, *([kp5] * n_in))


def _sample_select_body(q_ref, ks_ref, o_ref, *, nbp):
    q = jnp.broadcast_to(q_ref[0], (ATTN_HEADS, D_ATTN))
    qbd = jnp.where(_iota(q.shape, 1) // ATTN_HEAD_DIM == _iota(q.shape, 0), q, 0.0)
    s = jnp.dot(qbd, ks_ref[0], precision=HI, preferred_element_type=F32) * (1.0 / MOBA_BLOCK)
    lane = _iota(s.shape, 1)
    s = jnp.where(lane < nbp, s, -jnp.inf)
    top = jnp.zeros(s.shape, I32)
    for r in range(min(MOBA_TOPK, nbp)):
        m = jnp.max(s, axis=-1, keepdims=True)
        i = jnp.min(jnp.where(s == m, lane, LANES), axis=-1, keepdims=True)
        top = jnp.where(lane == r, i, top)
        s = jnp.where(lane == i, -jnp.inf, s)
    o_ref[0] = top


def _sample_select(q, ksum, nbp):
    db = q.shape[0]
    return pl.pallas_call(
        functools.partial(_sample_select_body, nbp=nbp),
        grid=(db,),
        in_specs=[pl.BlockSpec((1, 1, D_ATTN), lambda d: (d, 0, 0)),
                  pl.BlockSpec((1, D_ATTN, LANES), lambda d: (d, 0, 0))],
        out_specs=pl.BlockSpec((1, ATTN_HEADS, LANES), lambda d: (d, 0, 0)),
        out_shape=jax.ShapeDtypeStruct((db, ATTN_HEADS, LANES), I32),
        compiler_params=_cparams(("arbitrary",)),
        name="sample_select",
    )(q.reshape(db, 1, D_ATTN), ksum)


def _row_to_col(row):
    eye = _iota((LANES, LANES), 0) == _iota((LANES, LANES), 1)
    return jnp.sum(jnp.where(eye, jnp.broadcast_to(row, (LANES, LANES)), 0.0), axis=-1, keepdims=True)


def _col_to_row(col):
    eye = _iota((LANES, LANES), 0) == _iota((LANES, LANES), 1)
    return jnp.sum(jnp.where(eye, jnp.broadcast_to(col, (LANES, LANES)), 0.0), axis=0, keepdims=True)


def _sample_attend_body(pt_ref, top_ref, q_ref, kn_ref, vn_ref, kp_ref, vp_ref, o_ref, kbuf, vbuf, sem,
                        *, layer, n_pages, n_sel, ppb):
    d = pl.program_id(0)
    per_head = n_sel * ppb

    def copies(h, r, j):
        blk = top_ref[(d * ATTN_HEADS + h) * n_sel + r]
        phys = pt_ref[d * n_pages + blk * ppb + j]
        slot = h * per_head + r * ppb + j
        return (pltpu.make_async_copy(kp_ref.at[layer, phys, h], kbuf.at[slot], sem.at[0]),
                pltpu.make_async_copy(vp_ref.at[layer, phys, h], vbuf.at[slot], sem.at[1]))

    todo = [(h, r, j) for h in range(ATTN_HEADS) for r in range(n_sel) for j in range(ppb)]
    for hrj in todo:
        for c in copies(*hrj):
            c.start()
    q_row = q_ref[0] * (ATTN_HEAD_DIM ** -0.5)
    k_row = kn_ref[0]
    v_row = vn_ref[0]
    tiles = D_ATTN // LANES
    q_col = jnp.concatenate([_row_to_col(q_row[:, t * LANES:(t + 1) * LANES]) for t in range(tiles)], axis=0)
    head_of_lane = _iota((1, D_ATTN), 1) // ATTN_HEAD_DIM
    qk_own = q_row * k_row
    for hrj in todo:
        for c in copies(*hrj):
            c.wait()
    acc_cols = []
    p_own_row = jnp.zeros((1, D_ATTN), F32)
    l_row = jnp.ones((1, D_ATTN), F32)
    for h in range(ATTN_HEADS):
        qc = q_col[h * ATTN_HEAD_DIM:(h + 1) * ATTN_HEAD_DIM]
        base = h * per_head
        s_pages = [jnp.sum(kbuf[base + g] * qc, axis=0, keepdims=True) for g in range(per_head)]
        s_own = jnp.sum(jnp.where(head_of_lane == h, qk_own, 0.0), axis=-1, keepdims=True)
        m = s_own
        for s in s_pages:
            m = jnp.maximum(m, jnp.max(s, axis=-1, keepdims=True))
        p_own = jnp.exp(s_own - m)
        l = p_own
        acc = jnp.zeros(kbuf.shape[1:], F32)
        for g, s in enumerate(s_pages):
            p = jnp.exp(s - m)
            l = l + jnp.sum(p, axis=-1, keepdims=True)
            acc = acc + vbuf[base + g] * p
        acc_cols.append(jnp.sum(acc, axis=-1, keepdims=True))
        p_own_row = jnp.where(head_of_lane == h, p_own, p_own_row)
        l_row = jnp.where(head_of_lane == h, l, l_row)
    acc_col = jnp.concatenate(acc_cols, axis=0)
    acc_row = jnp.concatenate([_col_to_row(acc_col[t * LANES:(t + 1) * LANES]) for t in range(tiles)], axis=1)
    o_ref[0] = (acc_row + p_own_row * v_row) / l_row


def _sample_attend(q, k_new, v_new, kp5, vp5, pt_flat, top_flat, layer, n_pages, n_sel, ppb):
    db = q.shape[0]
    ps = kp5.shape[-1]
    row = pl.BlockSpec((1, 1, D_ATTN), lambda d, pt, top: (d, 0, 0))
    n_buf = ATTN_HEADS * n_sel * ppb
    out = pl.pallas_call(
        functools.partial(_sample_attend_body, layer=layer, n_pages=n_pages, n_sel=n_sel, ppb=ppb),
        grid_spec=pltpu.PrefetchScalarGridSpec(
            num_scalar_prefetch=2,
            grid=(db,),
            in_specs=[row, row, row, pl.BlockSpec(memory_space=pl.ANY), pl.BlockSpec(memory_space=pl.ANY)],
            out_specs=row,
            scratch_shapes=[pltpu.VMEM((n_buf, ATTN_HEAD_DIM, ps), F32), pltpu.VMEM((n_buf, ATTN_HEAD_DIM, ps), F32),
                            pltpu.SemaphoreType.DMA((2,))],
        ),
        out_shape=jax.ShapeDtypeStruct((db, 1, D_ATTN), F32),
        compiler_params=_cparams(("arbitrary",)),
        name="sample_attend",
    )(pt_flat, top_flat, q.reshape(db, 1, D_ATTN), k_new.reshape(db, 1, D_ATTN), v_new.reshape(db, 1, D_ATTN),
      kp5, vp5)
    return out.reshape(db, D_ATTN)


def _moba_sample(q, k_new, v_new, k_pool, v_pool, page_table, layer):
    db = q.shape[0]
    ps = k_pool.shape[2]
    n_pages = page_table.shape[1]
    ppb = MOBA_BLOCK // ps
    nbp = (n_pages * ps) // MOBA_BLOCK
    assert nbp >= 1 and (n_pages - nbp * ppb) * ps == 0, "the cached rows must fill whole MoBA blocks"
    n_sel = min(MOBA_TOPK, nbp)
    kp5 = jnp.transpose(k_pool, (0, 1, 3, 4, 2))
    vp5 = jnp.transpose(v_pool, (0, 1, 3, 4, 2))
    pt_flat = page_table.reshape(-1)
    ksum = _page_sums(kp5, pt_flat, layer, db, n_pages, ppb)
    top = _sample_select(q, ksum, nbp)
    top_flat = top[:, :, :n_sel].reshape(-1)
    return _sample_attend(q, k_new, v_new, kp5, vp5, pt_flat, top_flat, layer, n_pages, n_sel, ppb)


def _mlstm_step_body(qk_ref, vm_ref, if_ref, conv_ref, cw_ref, c_ref, n_ref, m_ref,
                     h_ref, c_out, n_out, m_out):
    tb = qk_ref.shape[0]
    w = cw_ref[...]
    lane = _iota((1, LANES), 1)

    def one(r):
        x = qk_ref[pl.ds(r, 1), :]
        cb = conv_ref[pl.ds(r, 1)][0]
        y = cb[0:1] * w[0:1]
        for j in range(1, CONV_W - 1):
            y = y + cb[j:j + 1] * w[j:j + 1]
        y = y + x * w[CONV_W - 1:CONV_W]
        qkc = _silu(y)
        gates = if_ref[pl.ds(r, 1), :]
        m_all = m_ref[pl.ds(r, 1), :]
        v_all = vm_ref[pl.ds(r, 1), :]
        for h in range(MLSTM_HEADS):
            sl = slice(h * MLSTM_HEAD_DIM, (h + 1) * MLSTM_HEAD_DIM)
            q = qkc[:, sl]
            k = qkc[:, D_MLSTM + h * MLSTM_HEAD_DIM:D_MLSTM + (h + 1) * MLSTM_HEAD_DIM] * (MLSTM_HEAD_DIM ** -0.5)
            v = v_all[:, sl]
            ig = gates[:, h:h + 1]
            fg = gates[:, MLSTM_HEADS + h:MLSTM_HEADS + h + 1]
            m_prev = m_all[:, h:h + 1]
            inter = _log_sigmoid(fg) + m_prev
            m_t = jnp.maximum(inter, ig)
            wgt = jnp.exp(ig - m_t)
            w0 = jnp.exp(inter - m_t)
            c_prev = c_ref[pl.ds(r, 1), h][0]
            n_prev = n_ref[pl.ds(r, 1), h]
            sqk = jnp.sum(q * k, axis=-1, keepdims=True) * wgt
            cq = _nt(jnp.broadcast_to(q, (SUBLANES, MLSTM_HEAD_DIM)), c_prev, precision=HI)[0:1]
            num = w0 * cq + sqk * v
            den = w0 * jnp.sum(n_prev * q, axis=-1, keepdims=True) + sqk
            h_ref[pl.ds(r, 1), sl] = num / jnp.maximum(jnp.abs(den), jnp.exp(-m_t))
            c_out[pl.ds(r, 1), h] = (w0 * c_prev + (wgt * _row_to_col(v)) * k)[None]
            n_out[pl.ds(r, 1), h] = w0 * n_prev + wgt * k
            m_all = jnp.where(lane == h, m_t, m_all)
        m_out[pl.ds(r, 1), :] = m_all

    for r in range(tb):
        one(r)


def _mlstm_step(qk_m, v_m, if_m, conv_state, conv_w, c, n, m_pad, tb):
    db = qk_m.shape[0]
    row = lambda w: pl.BlockSpec((tb, w), lambda i: (i, 0))
    c_spec = pl.BlockSpec((tb, MLSTM_HEADS, MLSTM_HEAD_DIM, MLSTM_HEAD_DIM), lambda i: (i, 0, 0, 0))
    n_spec = pl.BlockSpec((tb, MLSTM_HEADS, MLSTM_HEAD_DIM), lambda i: (i, 0, 0))
    return pl.pallas_call(
        _mlstm_step_body,
        grid=(db // tb,),
        in_specs=[row(2 * D_MLSTM), row(D_MLSTM), row(LANES),
                  pl.BlockSpec((tb, CONV_W - 1, 2 * D_MLSTM), lambda i: (i, 0, 0)),
                  pl.BlockSpec((CONV_W, 2 * D_MLSTM), lambda i: (0, 0)), c_spec, n_spec, row(LANES)],
        out_specs=[row(D_MLSTM), c_spec, n_spec, row(LANES)],
        out_shape=[jax.ShapeDtypeStruct((db, D_MLSTM), F32), jax.ShapeDtypeStruct(c.shape, F32),
                   jax.ShapeDtypeStruct(n.shape, F32), jax.ShapeDtypeStruct((db, LANES), F32)],
        compiler_params=_cparams(("arbitrary",)),
        name="mlstm_step",
    )(qk_m, v_m, if_m, conv_state, conv_w, c, n, m_pad)


def _sample_path(x, mod6, p, k_pool, v_pool, page_table, c0, n0, m0, conv0, layer):
    db, t, d = x.shape
    assert t == 1, "one new token per sequence"
    assert db % SUBLANES == 0
    xs = x.reshape(1, db, d)
    sh1, sc1, g1, sh2, sc2, g2 = [m.reshape(1, db, d) for m in mod6]
    q, k, v, _, _, _, qk_m, v_m, o_m, if_m, _ = _front(xs, sc1, sh1, p["norm1_g"], p["w_in"], p["b_if"], db)
    q, k, v, qk_m, v_m, if_m = [a[0] for a in (q, k, v, qk_m, v_m, if_m)]
    attn = _moba_sample(q, k, v, k_pool, v_pool, page_table, layer)
    m_pad = jnp.pad(m0, ((0, 0), (0, LANES - MLSTM_HEADS)))
    h_s, c_s, n_s, m_s = _mlstm_step(qk_m, v_m, if_m, conv0, p["conv_w"], c0, n0, m_pad, SUBLANES)
    y = _layer_tail(xs, (g1, sh2, sc2, g2), attn[None], h_s[None], o_m, p["attn_norm_g"], p["mlstm_norm_g"],
                    p["w_out"], p["norm2_g"], p["peer_w_query"], p["peer_sub_keys"], p["uv"], p["normf_g"],
                    db, min(64, db))
    conv_s = jnp.concatenate([conv0[:, 1:], qk_m[:, None, :]], axis=1)
    return y.reshape(db, t, d), k, v, c_s, n_s, m_s[:, :MLSTM_HEADS], conv_s


def kernel(x_prompt, x_sample, cache_k_pool, cache_v_pool, page_table, state_mlstm_C, state_mlstm_n, state_mlstm_m, state_conv, c_prompt, c_sample, w_ada, b_ada, norm1_g, w_in, b_if, conv_w, attn_norm_g, mlstm_norm_g, w_out, norm2_g, peer_w_query, peer_sub_keys, peer_u, peer_v, normf_g):
    b, s, d = x_prompt.shape
    db = x_sample.shape[0]
    l = 0
    p = dict(norm1_g=norm1_g[l], w_in=w_in[l], b_if=b_if[l], conv_w=conv_w[l], attn_norm_g=attn_norm_g[l],
             mlstm_norm_g=mlstm_norm_g[l], w_out=w_out[l], norm2_g=norm2_g[l], peer_w_query=peer_w_query[l],
             peer_sub_keys=peer_sub_keys[l], normf_g=normf_g,
             uv=jnp.concatenate([_pack_bf16_pairs(peer_u[l]), _pack_bf16_pairs(peer_v[l])], axis=1)[:, None, :])
    rows = b + db
    pad = -rows % SUBLANES
    c_all = jnp.pad(jnp.concatenate([c_prompt, c_sample], axis=0), ((0, pad), (0, 0)))
    mod = _ada(c_all, w_ada[l], b_ada[l])
    mod_p = [mod[:b, i * d:(i + 1) * d] for i in range(6)]
    mod_s = [mod[b:rows, i * d:(i + 1) * d] for i in range(6)]
    y_p, k_p, v_p, c_p, n_p, m_p, conv_p = _prompt_path(x_prompt, mod_p, p)
    y_s, k_s, v_s, c_s, n_s, m_s, conv_s = _sample_path(
        x_sample, mod_s, p, cache_k_pool, cache_v_pool, page_table,
        state_mlstm_C[l], state_mlstm_n[l], state_mlstm_m[l], state_conv[l], l)
    hp = (b, s, ATTN_HEADS, ATTN_HEAD_DIM)
    hs = (db, 1, ATTN_HEADS, ATTN_HEAD_DIM)
    return (y_p, y_s, k_p.reshape(hp)[None], v_p.reshape(hp)[None], k_s.reshape(hs)[None], v_s.reshape(hs)[None],
            c_p[None], n_p[None], m_p[None], conv_p[None], c_s[None], n_s[None], m_s[None], conv_s[None])
```

```python
import functools

import jax
import jax.numpy as jnp
from jax import lax
from jax.experimental import pallas as pl
from jax.experimental.pallas import tpu as pltpu

F32 = jnp.float32
BF16 = jnp.bfloat16
I32 = jnp.int32
HI = lax.Precision.HIGHEST

SUBLANES = 8
LANES = 128
VMEM_LIMIT_BYTES = 56 * 1024 * 1024

D_MODEL = 1024
D_ATTN = 512
D_MLSTM = 512
ATTN_HEADS = 8
ATTN_HEAD_DIM = 64
MLSTM_HEADS = 4
MLSTM_HEAD_DIM = 128
MOBA_BLOCK = 256
MOBA_TOPK = 3
CONV_W = 4
PEER_HEADS = 8
PEER_N_KEYS = 128
PEER_TOPK = 16
PEER_HALF = 128
EPS = 1e-6
N_GATES = 2 * MLSTM_HEADS
C_Q, C_K, C_V, C_QK, C_VM, C_OM, C_IF = 0, 512, 1024, 1536, 2560, 3072, 3584
NEG = -1e30


def _cparams(sem):
    return pltpu.CompilerParams(dimension_semantics=sem, vmem_limit_bytes=VMEM_LIMIT_BYTES)


def _nt(a, b, **kw):
    return lax.dot_general(a, b, (((1,), (1,)), ((), ())), preferred_element_type=F32, **kw)


def _tn(a, b, **kw):
    return lax.dot_general(a, b, (((0,), (0,)), ((), ())), preferred_element_type=F32, **kw)


def _silu(x):
    return x * jax.nn.sigmoid(x)


def _log_sigmoid(x):
    return jnp.minimum(x, 0.0) - jnp.log1p(jnp.exp(-jnp.abs(x)))


def _iota(shape, dim):
    return lax.broadcasted_iota(I32, shape, dim)


def _ada_body(c_ref, w_ref, b_ref, o_ref):
    o_ref[...] = jnp.dot(_silu(c_ref[...]), w_ref[...], precision=HI, preferred_element_type=F32) + b_ref[...]


def _ada(c, w_ada, b_ada):
    rows, d = c.shape
    n = w_ada.shape[1]
    tn = 512
    return pl.pallas_call(
        _ada_body,
        grid=(n // tn,),
        in_specs=[pl.BlockSpec((rows, d), lambda j: (0, 0)),
                  pl.BlockSpec((d, tn), lambda j: (0, j)),
                  pl.BlockSpec((1, tn), lambda j: (0, j))],
        out_specs=pl.BlockSpec((rows, tn), lambda j: (0, j)),
        out_shape=jax.ShapeDtypeStruct((rows, n), F32),
        compiler_params=_cparams(("arbitrary",)),
        name="ada_mod",
    )(c, w_ada, b_ada.reshape(1, n))


def _front_body(x_ref, sc_ref, sh_ref, g_ref, w_ref, wif_ref, wift_ref, bif_ref, bift_ref,
                q_ref, k_ref, v_ref, qb_ref, kb_ref, vb_ref, qk_ref, vm_ref, om_ref, if_ref, ift_ref):
    x = x_ref[0]
    h = x * lax.rsqrt(jnp.mean(x * x, axis=-1, keepdims=True) + EPS) * g_ref[...]
    h = h * (1.0 + sc_ref[0]) + sh_ref[0]
    hb = h.astype(BF16)

    def seg(lo, hi):
        return jnp.dot(hb, w_ref[:, lo:hi], preferred_element_type=F32)

    q = seg(C_Q, C_K)
    q_ref[0] = q
    qb_ref[0] = (q * (ATTN_HEAD_DIM ** -0.5)).astype(BF16)
    k = seg(C_K, C_V)
    k_ref[0] = k
    kb_ref[0] = k.astype(BF16)
    v = seg(C_V, C_QK)
    v_ref[0] = v
    vb_ref[0] = v.astype(BF16)
    qk_ref[0] = seg(C_QK, C_VM)
    vm_ref[0] = seg(C_VM, C_OM)
    om_ref[0] = seg(C_OM, C_IF)
    if_ref[0] = jnp.dot(h, wif_ref[...], precision=HI, preferred_element_type=F32) + bif_ref[...]
    ift_ref[0] = _nt(wift_ref[...], h, precision=HI) + bift_ref[...]


def _front(x, sc, sh, norm_g, w_in, b_if, tm):
    b, s, d = x.shape
    per_row = sc.shape[1] != 1
    mod_spec = (pl.BlockSpec((1, tm, d), lambda bi, i: (bi, i, 0)) if per_row
                else pl.BlockSpec((1, 1, d), lambda bi, i: (bi, 0, 0)))
    w_main = w_in[:, :C_IF].astype(BF16)
    w_if = jnp.pad(w_in[:, C_IF:], ((0, 0), (0, LANES - N_GATES)))
    w_ift = w_in[:, C_IF:].T
    b_row = jnp.pad(b_if, (0, LANES - N_GATES)).reshape(1, LANES)
    b_col = b_if.reshape(N_GATES, 1)
    const = lambda shape: pl.BlockSpec(shape, lambda bi, i: (0,) * len(shape))
    tok = lambda w: pl.BlockSpec((1, tm, w), lambda bi, i: (bi, i, 0))
    out_shapes = [
        jax.ShapeDtypeStruct((b, s, D_ATTN), F32),
        jax.ShapeDtypeStruct((b, s, D_ATTN), F32),
        jax.ShapeDtypeStruct((b, s, D_ATTN), F32),
        jax.ShapeDtypeStruct((b, s, D_ATTN), BF16),
        jax.ShapeDtypeStruct((b, s, D_ATTN), BF16),
        jax.ShapeDtypeStruct((b, s, D_ATTN), BF16),
        jax.ShapeDtypeStruct((b, s, 2 * D_MLSTM), F32),
        jax.ShapeDtypeStruct((b, s, D_MLSTM), F32),
        jax.ShapeDtypeStruct((b, s, D_MLSTM), F32),
        jax.ShapeDtypeStruct((b, s, LANES), F32),
        jax.ShapeDtypeStruct((b, N_GATES, s), F32),
    ]
    out_specs = [tok(D_ATTN)] * 6 + [tok(2 * D_MLSTM), tok(D_MLSTM), tok(D_MLSTM), tok(LANES),
                                     pl.BlockSpec((1, N_GATES, tm), lambda bi, i: (bi, 0, i))]
    return pl.pallas_call(
        _front_body,
        grid=(b, s // tm),
        in_specs=[tok(d), mod_spec, mod_spec, const((1, d)), const((d, C_IF)), const((d, LANES)),
                  const((N_GATES, d)), const((1, LANES)), const((N_GATES, 1))],
        out_specs=out_specs,
        out_shape=out_shapes,
        compiler_params=_cparams(("arbitrary", "arbitrary")),
        name="front",
    )(x, sc, sh, norm_g.reshape(1, d), w_main, w_if, w_ift, b_row, b_col)


def _km_body(k_ref, km_ref, *, hb):
    n = pl.program_id(1)

    @pl.when(n == 0)
    def _():
        km_ref[...] = jnp.zeros_like(km_ref)

    lw = km_ref.shape[2]
    ones = jnp.full((MOBA_BLOCK, lw), 1.0 / MOBA_BLOCK, F32)
    mean_b = _tn(k_ref[0], ones, precision=HI)
    row = _iota((D_ATTN, lw), 0)
    col = _iota((D_ATTN, lw), 1)
    km_ref[0] += jnp.where(col == (row // ATTN_HEAD_DIM) * hb + n, mean_b, 0.0)


def _block_means(k, nb, hb):
    lw = ATTN_HEADS * hb
    b = k.shape[0]
    return pl.pallas_call(
        functools.partial(_km_body, hb=hb),
        grid=(b, nb),
        in_specs=[pl.BlockSpec((1, MOBA_BLOCK, D_ATTN), lambda bi, n: (bi, n, 0))],
        out_specs=pl.BlockSpec((1, D_ATTN, lw), lambda bi, n: (bi, 0, 0)),
        out_shape=jax.ShapeDtypeStruct((b, D_ATTN, lw), F32),
        compiler_params=_cparams(("arbitrary", "arbitrary")),
        name="moba_block_means",
    )(k)


def _sel_body(q_ref, km_ref, o_ref, *, hb, n_top):
    own = pl.program_id(1)
    s = jnp.dot(q_ref[0], km_ref[0], precision=HI, preferred_element_type=F32)
    shape = s.shape
    lw = shape[1]
    lane = _iota(shape, 1)
    blk = lane & (hb - 1)
    head = lane >> (hb.bit_length() - 1)
    valid = (blk < own) & (head < ATTN_HEADS)
    sel = jnp.zeros(shape, jnp.bool_)
    for h in range(ATTN_HEADS):
        sm = jnp.where(valid & (head == h), s, -jnp.inf)
        for _ in range(n_top):
            m = jnp.max(sm, axis=-1, keepdims=True)
            i = jnp.min(jnp.where(sm == m, lane, lw), axis=-1, keepdims=True)
            hit = lane == i
            sel = sel | (hit & (m > -jnp.inf))
            sm = jnp.where(hit, -jnp.inf, sm)
    o_ref[0] = jnp.where(sel, 0.0, NEG)


def _moba_select(q, km, nb, hb):
    b, s, _ = q.shape
    lw = km.shape[2]
    return pl.pallas_call(
        functools.partial(_sel_body, hb=hb, n_top=min(MOBA_TOPK, nb)),
        grid=(b, s // MOBA_BLOCK),
        in_specs=[pl.BlockSpec((1, MOBA_BLOCK, D_ATTN), lambda bi, i: (bi, i, 0)),
                  pl.BlockSpec((1, D_ATTN, lw), lambda bi, i: (bi, 0, 0))],
        out_specs=pl.BlockSpec((1, MOBA_BLOCK, lw), lambda bi, i: (bi, i, 0)),
        out_shape=jax.ShapeDtypeStruct((b, s, lw), F32),
        compiler_params=_cparams(("arbitrary", "arbitrary")),
        name="moba_select",
    )(q, km)


def _attn_body(q_ref, k_ref, v_ref, sb_ref, lid_ref, lidm_ref, o_ref, *, hb):
    own = pl.program_id(2)
    tq = MOBA_BLOCK
    tiles = q_ref.shape[2] // LANES
    n_heads = 2 * tiles
    lane = _iota((tq, LANES), 1)
    low = lane < ATTN_HEAD_DIM
    sb = sb_ref[0]
    one = jnp.ones((2 * tq, LANES), BF16)
    zero = jnp.zeros((2 * tq, LANES), BF16)
    lane_b = lid_ref[...]
    lane_m = lidm_ref[...]

    def own_lanes(is_low, rows):
        ids = lane_b[:rows]
        return ids < ATTN_HEAD_DIM if is_low else ids >= ATTN_HEAD_DIM

    qs, mine, bias0 = [], [], []
    for hh in range(n_heads):
        t = hh // 2
        q = q_ref[0, :, t * LANES:(t + 1) * LANES]
        is_low = hh % 2 == 0
        first = ATTN_HEAD_DIM if is_low else 0
        tile, off = divmod(hh * hb, LANES)
        bias = sb[:, tile * LANES:(tile + 1) * LANES]
        shift = (first - off) % LANES
        if shift:
            bias = pltpu.roll(bias, shift, 1)
        qs.append(jnp.where(own_lanes(is_low, tq), q, bias.astype(BF16)))
        mine.append(is_low)
        bias0.append(first)
    row = _iota((tq, tq), 0)
    col = _iota((tq, tq), 1)

    def attend(j, state, is_own, n_blk):
        rows = n_blk * tq
        start = pl.multiple_of(j * tq, rows)
        new = []
        for hh in range(n_heads):
            sl = slice((hh // 2) * LANES, (hh // 2 + 1) * LANES)
            kj = k_ref[0, pl.ds(start, rows), sl]
            vj = v_ref[0, pl.ds(start, rows), sl]
            m, acc = state[hh]
            keep = own_lanes(mine[hh], rows)
            if is_own:
                s = _nt(qs[hh], jnp.where(keep, kj, zero[:rows]))
                s = jnp.where(col <= row, s, -jnp.inf)
            else:
                hot = lane_m[:rows] == (bias0[hh] + j).astype(F32).astype(BF16)
                s = _nt(qs[hh], jnp.where(keep, kj, jnp.where(hot, one[:rows], zero[:rows])))
            m_new = jnp.maximum(m, jnp.max(s, axis=-1, keepdims=True))
            p = jnp.exp(s - m_new)
            acc = jnp.exp(m - m_new) * acc + jnp.dot(p.astype(BF16), jnp.where(keep, vj, one[:rows]),
                                                      preferred_element_type=F32)
            new.append((m_new, acc))
        return tuple(new)

    init = tuple((jnp.full((tq, 1), -jnp.inf, F32), jnp.zeros((tq, LANES), F32)) for _ in range(n_heads))
    state = attend(own, init, True, 1)
    state = lax.fori_loop(0, own // 2, lambda i, st: attend(2 * i, st, False, 2), state)
    state = lax.cond(own % 2 == 1, lambda st: attend(own - 1, st, False, 1), lambda st: st, state)
    for t in range(tiles):
        outs = []
        for hh in (2 * t, 2 * t + 1):
            acc = state[hh][1]
            outs.append(acc / pltpu.roll(acc, ATTN_HEAD_DIM, 1))
        o_ref[0, :, t * LANES:(t + 1) * LANES] = jnp.where(low, outs[0], outs[1])


ATTN_LANES_PER_STEP = 2 * LANES


def _moba_attend(qb, kb, vb, selbias, hb):
    b, s, _ = qb.shape
    gw = ATTN_LANES_PER_STEP
    groups = D_ATTN // gw
    sw = selbias.shape[2] // groups
    assert sw % LANES == 0
    lane_id = jnp.broadcast_to(jnp.arange(LANES, dtype=F32), (2 * MOBA_BLOCK, LANES))
    second_block = (jnp.arange(2 * MOBA_BLOCK, dtype=F32) >= MOBA_BLOCK).astype(F32)[:, None]
    return pl.pallas_call(
        functools.partial(_attn_body, hb=hb),
        grid=(b, groups, s // MOBA_BLOCK),
        in_specs=[pl.BlockSpec((1, MOBA_BLOCK, gw), lambda bi, p, i: (bi, i, p)),
                  pl.BlockSpec((1, s, gw), lambda bi, p, i: (bi, 0, p)),
                  pl.BlockSpec((1, s, gw), lambda bi, p, i: (bi, 0, p)),
                  pl.BlockSpec((1, MOBA_BLOCK, sw), lambda bi, p, i: (bi, i, p)),
                  pl.BlockSpec((2 * MOBA_BLOCK, LANES), lambda bi, p, i: (0, 0)),
                  pl.BlockSpec((2 * MOBA_BLOCK, LANES), lambda bi, p, i: (0, 0))],
        out_specs=pl.BlockSpec((1, MOBA_BLOCK, gw), lambda bi, p, i: (bi, i, p)),
        out_shape=jax.ShapeDtypeStruct((b, s, D_ATTN), F32),
        compiler_params=_cparams(("arbitrary", "arbitrary", "arbitrary")),
        name="moba_attend",
    )(qb, kb, vb, selbias, lane_id.astype(BF16), (lane_id - second_block).astype(BF16))


MIN_HEAD_STRIDE = 32


def _moba_prompt(q, k, qb, kb, vb):
    s = q.shape[1]
    assert s % MOBA_BLOCK == 0
    nb = s // MOBA_BLOCK
    assert nb & (nb - 1) == 0, "block count must be a power of two"
    hb = max(nb, MIN_HEAD_STRIDE)
    assert hb <= ATTN_HEAD_DIM, "the per-block bias must fit in the free half of a lane tile"
    km = _block_means(k, nb, hb)
    selbias = _moba_select(q, km, nb, hb)
    return _moba_attend(qb, kb, vb, selbias, hb)


def _mlstm_body(qk_ref, vm_ref, if_ref, ift_ref, cw_ref, h_ref, c_ref, n_ref, m_ref, prev_scr):
    i = pl.program_id(1)
    L = qk_ref.shape[1]

    @pl.when(i == 0)
    def _():
        prev_scr[...] = jnp.zeros_like(prev_scr)
        c_ref[...] = jnp.zeros_like(c_ref)
        n_ref[...] = jnp.zeros_like(n_ref)
        m_ref[...] = jnp.zeros_like(m_ref)

    x = qk_ref[0]
    prev = prev_scr[...]
    w = cw_ref[...]
    rows = _iota(x.shape, 0)
    y = x * w[CONV_W - 1:CONV_W]
    for sft in range(1, CONV_W):
        xs = jnp.where(rows >= sft, pltpu.roll(x, sft, 0), pltpu.roll(prev, sft, 0))
        y = y + xs * w[CONV_W - 1 - sft:CONV_W - sft]
    prev_scr[...] = x
    qkc = _silu(y)

    gcol = if_ref[0]
    grow = ift_ref[0]
    tri_r = _iota((L, L), 0)
    tri_c = _iota((L, L), 1)
    lower = tri_c <= tri_r
    b_col = jnp.dot(lower.astype(F32), _log_sigmoid(gcol), precision=HI, preferred_element_type=F32)
    b_row = jnp.dot(_log_sigmoid(grow), (tri_r <= tri_c).astype(F32), precision=HI, preferred_element_type=F32)
    m_all = m_ref[0]
    lane = _iota(m_all.shape, 1)
    for h in range(MLSTM_HEADS):
        sl = slice(h * MLSTM_HEAD_DIM, (h + 1) * MLSTM_HEAD_DIM)
        qh = qkc[:, sl]
        kh = qkc[:, D_MLSTM + h * MLSTM_HEAD_DIM:D_MLSTM + (h + 1) * MLSTM_HEAD_DIM] * (MLSTM_HEAD_DIM ** -0.5)
        vh = vm_ref[0][:, sl]
        fh = MLSTM_HEADS + h
        bc = b_col[:, fh:fh + 1]
        br = b_row[fh:fh + 1, :]
        ic = gcol[:, h:h + 1]
        ir = grow[h:h + 1, :]
        m_prev = m_all[:, h:h + 1]
        c_prev = c_ref[0, h]
        n_prev = n_ref[0, h:h + 1, :]
        dmat = jnp.where(lower, bc - br + ir, -jnp.inf)
        inter = bc + m_prev
        m_t = jnp.maximum(inter, jnp.max(dmat, axis=-1, keepdims=True))
        wmat = jnp.exp(dmat - m_t)
        w0 = jnp.exp(inter - m_t)
        qb, kb, vb = qh.astype(BF16), kh.astype(BF16), vh.astype(BF16)
        sqk = _nt(qb, kb) * wmat
        num = w0 * _nt(qb, c_prev.astype(BF16)) + jnp.dot(sqk.astype(BF16), vb, preferred_element_type=F32)
        den = w0 * jnp.sum(qh * n_prev, axis=-1, keepdims=True) + jnp.sum(sqk, axis=-1, keepdims=True)
        h_ref[0, :, sl] = num / jnp.maximum(jnp.abs(den), jnp.exp(-m_t))
        m_new = m_t[L - 1:L, :]
        b_last = bc[L - 1:L, :]
        g = jnp.exp(b_last - bc + ic - m_new)
        decay = jnp.exp(b_last + m_prev - m_new)
        c_ref[0, h] = decay * c_prev + _tn((g * vh).astype(BF16), kb)
        n_ref[0, h:h + 1, :] = decay * n_prev + jnp.sum(g * kh, axis=0, keepdims=True)
        m_all = jnp.where(lane == h, m_new, m_all)
    m_ref[0] = m_all


def _mlstm_prompt(qk_m, v_m, if_m, if_t, conv_w, chunk):
    b, s, _ = qk_m.shape
    tok = lambda w: pl.BlockSpec((1, chunk, w), lambda bi, i: (bi, i, 0))
    return pl.pallas_call(
        _mlstm_body,
        grid=(b, s // chunk),
        in_specs=[tok(2 * D_MLSTM), tok(D_MLSTM), tok(LANES),
                  pl.BlockSpec((1, N_GATES, chunk), lambda bi, i: (bi, 0, i)),
                  pl.BlockSpec((CONV_W, 2 * D_MLSTM), lambda bi, i: (0, 0))],
        out_specs=[tok(D_MLSTM),
                   pl.BlockSpec((1, MLSTM_HEADS, MLSTM_HEAD_DIM, MLSTM_HEAD_DIM), lambda bi, i: (bi, 0, 0, 0)),
                   pl.BlockSpec((1, MLSTM_HEADS, MLSTM_HEAD_DIM), lambda bi, i: (bi, 0, 0)),
                   pl.BlockSpec((1, 1, LANES), lambda bi, i: (bi, 0, 0))],
        out_shape=[jax.ShapeDtypeStruct((b, s, D_MLSTM), F32),
                   jax.ShapeDtypeStruct((b, MLSTM_HEADS, MLSTM_HEAD_DIM, MLSTM_HEAD_DIM), F32),
                   jax.ShapeDtypeStruct((b, MLSTM_HEADS, MLSTM_HEAD_DIM), F32),
                   jax.ShapeDtypeStruct((b, 1, LANES), F32)],
        scratch_shapes=[pltpu.VMEM((chunk, 2 * D_MLSTM), F32)],
        compiler_params=_cparams(("arbitrary", "arbitrary")),
        name="mlstm_prompt",
    )(qk_m, v_m, if_m, if_t, conv_w)


def _back_body(x_ref, a_ref, hm_ref, om_ref, g1_ref, sc_ref, sh_ref, ag_ref, mg_ref, wo_ref, n2_ref, wq_ref,
               x1_ref, h2_ref, pq_ref):
    a = a_ref[0]
    tm = a.shape[0]
    lane = _iota((tm, LANES), 1)
    low = lane < ATTN_HEAD_DIM
    proj = jnp.zeros((tm, D_MODEL), F32)
    for t in range(D_ATTN // LANES):
        sl = slice(t * LANES, (t + 1) * LANES)
        at = a[:, sl]
        sq = at * at
        s_lo = jnp.sum(jnp.where(low, sq, 0.0), axis=-1, keepdims=True)
        s_hi = jnp.sum(jnp.where(low, 0.0, sq), axis=-1, keepdims=True)
        scale = jnp.where(low, lax.rsqrt(s_lo / ATTN_HEAD_DIM + EPS), lax.rsqrt(s_hi / ATTN_HEAD_DIM + EPS))
        an = at * scale * ag_ref[:, sl]
        proj = proj + jnp.dot(an.astype(BF16), wo_ref[sl, :], preferred_element_type=F32)
    hm = hm_ref[0]
    om = om_ref[0]
    for t in range(MLSTM_HEADS):
        sl = slice(t * LANES, (t + 1) * LANES)
        ht = hm[:, sl]
        hn = ht * lax.rsqrt(jnp.mean(ht * ht, axis=-1, keepdims=True) + EPS) * mg_ref[:, sl]
        mm = jax.nn.sigmoid(om[:, sl]) * hn
        proj = proj + jnp.dot(mm.astype(BF16), wo_ref[D_ATTN + t * LANES:D_ATTN + (t + 1) * LANES, :],
                              preferred_element_type=F32)
    x1 = x_ref[0] + g1_ref[0] * proj
    x1_ref[0] = x1
    h2 = x1 * lax.rsqrt(jnp.mean(x1 * x1, axis=-1, keepdims=True) + EPS) * n2_ref[...]
    h2 = h2 * (1.0 + sc_ref[0]) + sh_ref[0]
    h2_ref[0] = h2
    pq_ref[0] = jnp.dot(h2.astype(BF16), wq_ref[...], preferred_element_type=F32)


def _back(x, attn, h_m, o_m, g1, sc2, sh2, attn_g, mlstm_g, w_out, norm2_g, w_query, tm):
    b, s, d = x.shape
    per_row = g1.shape[1] != 1
    mod_spec = (pl.BlockSpec((1, tm, d), lambda bi, i: (bi, i, 0)) if per_row
                else pl.BlockSpec((1, 1, d), lambda bi, i: (bi, 0, 0)))
    nq = w_query.shape[1]
    const = lambda shape: pl.BlockSpec(shape, lambda bi, i: (0,) * len(shape))
    tok = lambda w: pl.BlockSpec((1, tm, w), lambda bi, i: (bi, i, 0))
    return pl.pallas_call(
        _back_body,
        grid=(b, s // tm),
        in_specs=[tok(d), tok(D_ATTN), tok(D_MLSTM), tok(D_MLSTM), mod_spec, mod_spec, mod_spec,
                  const((1, D_ATTN)), const((1, D_MLSTM)), const((d, d)), const((1, d)), const((d, nq))],
        out_specs=[tok(d), tok(d), tok(nq)],
        out_shape=[jax.ShapeDtypeStruct((b, s, d), F32), jax.ShapeDtypeStruct((b, s, d), F32),
                   jax.ShapeDtypeStruct((b, s, nq), F32)],
        compiler_params=_cparams(("arbitrary", "arbitrary")),
        name="back",
    )(x, attn, h_m, o_m, g1, sc2, sh2, attn_g.reshape(1, D_ATTN), mlstm_g.reshape(1, D_MLSTM),
      w_out.astype(BF16), norm2_g.reshape(1, d), w_query.astype(BF16))


def _peer_route_body(pq_ref, sk_ref, idx_ref, gate_ref):
    tp = pq_ref.shape[0]
    k = PEER_TOPK
    n_exp = PEER_N_KEYS * PEER_N_KEYS
    out_row = _iota((k, tp), 0)

    def extract(score, tag, fill):
        vals = jnp.zeros((k, tp), F32)
        tags = jnp.zeros((k, tp), F32)
        for r in range(k):
            m = jnp.max(score, axis=0, keepdims=True)
            t = jnp.min(jnp.where(score == m, tag, fill), axis=0, keepdims=True)
            vals = jnp.where(out_row == r, m, vals)
            tags = jnp.where(out_row == r, t, tags)
            score = jnp.where(tag == t, -jnp.inf, score)
        return vals, tags

    key_row = _iota((PEER_N_KEYS, tp), 0).astype(F32)
    pos = _iota((k * k, tp), 0).astype(F32)
    gates, ids = [], []
    for h in range(PEER_HEADS):
        halves = []
        for p in range(2):
            c0 = (2 * h + p) * PEER_HALF
            qh = pq_ref[:, c0:c0 + PEER_HALF].astype(BF16)
            s = _nt(sk_ref[2 * h + p].astype(BF16), qh)
            halves.append(extract(s, key_row, float(PEER_N_KEYS)))
        (v0, i0), (v1, i1) = halves
        cand = jnp.concatenate([v0[a:a + 1] + v1 for a in range(k)], axis=0)
        eid = jnp.concatenate([i0[a:a + 1] * PEER_N_KEYS + i1 for a in range(k)], axis=0)
        comb = pos * n_exp + eid
        c_s, c_key = extract(cand, comb, float(k * k * n_exp))
        e = jnp.exp(c_s - c_s[0:1])
        gates.append(e / jnp.sum(e, axis=0, keepdims=True))
        ids.append(c_key)
    gate_ref[...] = jnp.concatenate(gates, axis=0).T
    keys = jnp.concatenate(ids, axis=0).T.astype(I32)
    idx_ref[...] = keys & (n_exp - 1)


def _peer_route(pq, sub_keys):
    n = pq.shape[0]
    tp = LANES
    n_pad = -(-n // tp) * tp
    pq = jnp.pad(pq, ((0, n_pad - n), (0, 0)))
    sk = sub_keys.reshape(PEER_HEADS * 2, PEER_N_KEYS, PEER_HALF)
    idx, gate = pl.pallas_call(
        _peer_route_body,
        grid=(n_pad // tp,),
        in_specs=[pl.BlockSpec((tp, pq.shape[1]), lambda i: (i, 0)),
                  pl.BlockSpec(sk.shape, lambda i: (0, 0, 0))],
        out_specs=[pl.BlockSpec((tp, LANES), lambda i: (i, 0)), pl.BlockSpec((tp, LANES), lambda i: (i, 0))],
        out_shape=[jax.ShapeDtypeStruct((n_pad, LANES), I32), jax.ShapeDtypeStruct((n_pad, LANES), F32)],
        compiler_params=_cparams(("arbitrary",)),
        name="peer_route",
    )(pq, sk)
    return idx[:n], gate[:n]


N_SEL = PEER_HEADS * PEER_TOPK
EXPERT_SLOTS = 8


def _pack_bf16_pairs(a):
    half = a.shape[1] // 2
    b = lax.bitcast_convert_type(a.astype(BF16), jnp.uint16).astype(jnp.uint32)
    return (b[:, :half] << 16) | b[:, half:]


def _peer_expert_body(idx_ref, gate_ref, h2_ref, x1_ref, g2_ref, nf_ref, uv_ref, y_ref, *scratch):
    n_slots = EXPERT_SLOTS
    bufs = scratch[:n_slots]
    sem, peer_scr = scratch[n_slots], scratch[n_slots + 1]
    tt = h2_ref.shape[1]
    half = D_MODEL // 2

    def issue(t, slot):
        for k in range(N_SEL):
            e = idx_ref[0, 0, t * N_SEL + k]
            pltpu.make_async_copy(uv_ref.at[e], bufs[slot].at[pl.ds(k, 1)],
                                  sem.at[slot]).start(priority=k % 2)

    def wait_all(slot):
        pltpu.make_async_copy(uv_ref.at[pl.ds(0, N_SEL), 0], bufs[slot], sem.at[slot]).wait()

    rr = _iota((N_SEL, LANES), 0)
    cc = _iota((N_SEL, LANES), 1)
    diag = rr == cc

    def unpack(words):
        hi = lax.bitcast_convert_type(words & jnp.uint32(0xFFFF0000), F32)
        lo = lax.bitcast_convert_type(words << 16, F32)
        return hi, lo

    def compute(t, slot):
        x = h2_ref[0, pl.ds(t, 1), :]
        u_hi, u_lo = unpack(bufs[slot][:, 0:half])
        s = jnp.sum(u_hi * x[:, 0:half] + u_lo * x[:, half:D_MODEL], axis=-1, keepdims=True)
        act = 0.5 * s * (1.0 + lax.erf(s * (0.5 ** 0.5)))
        g_row = gate_ref[0, pl.ds(t, 1), :]
        g_col = jnp.sum(jnp.where(diag, jnp.broadcast_to(g_row, (N_SEL, LANES)), 0.0), axis=-1, keepdims=True)
        w = g_col * act
        v_hi, v_lo = unpack(bufs[slot][:, half:2 * half])
        peer_scr[pl.ds(t, 1), 0:half] = jnp.sum(v_hi * w, axis=0, keepdims=True)
        peer_scr[pl.ds(t, 1), half:D_MODEL] = jnp.sum(v_lo * w, axis=0, keepdims=True)

    ahead = n_slots - 2
    for t in range(ahead):
        issue(t, t)

    def token_group(t0, last):
        for sl in range(n_slots):
            if not last or sl + ahead < n_slots:
                issue(t0 + sl + ahead, (sl + ahead) % n_slots)
            wait_all(sl)
            compute(t0 + sl, sl)

    def steady(i, carry):
        token_group(n_slots * i, False)
        return carry

    n_groups = tt // n_slots
    lax.fori_loop(0, n_groups - 1, steady, 0)
    token_group(n_slots * (n_groups - 1), True)
    y = x1_ref[0] + g2_ref[0] * peer_scr[...]
    y_ref[0] = y * lax.rsqrt(jnp.mean(y * y, axis=-1, keepdims=True) + EPS) * nf_ref[...]


def _peer_experts(idx, gate, h2, x1, g2, normf_g, uv, tt):
    b, s, d = h2.shape
    n_slots = EXPERT_SLOTS
    assert tt % n_slots == 0 and tt >= n_slots and s % tt == 0
    nt = s // tt
    per_row = g2.shape[1] != 1
    mod_spec = (pl.BlockSpec((1, tt, d), lambda bi, i: (bi, i, 0)) if per_row
                else pl.BlockSpec((1, 1, d), lambda bi, i: (bi, 0, 0)))
    tok = lambda w: pl.BlockSpec((1, tt, w), lambda bi, i: (bi, i, 0))
    idx3 = idx.reshape(b * nt, 1, tt * N_SEL)
    return pl.pallas_call(
        _peer_expert_body,
        grid=(b, nt),
        in_specs=[pl.BlockSpec((1, 1, tt * N_SEL), lambda bi, i: (bi * nt + i, 0, 0), memory_space=pltpu.SMEM),
                  tok(LANES), tok(d), tok(d), mod_spec,
                  pl.BlockSpec((1, d), lambda bi, i: (0, 0)),
                  pl.BlockSpec(memory_space=pl.ANY)],
        out_specs=tok(d),
        out_shape=jax.ShapeDtypeStruct((b, s, d), F32),
        scratch_shapes=[pltpu.VMEM((N_SEL, uv.shape[2]), jnp.uint32)] * n_slots + [
                        pltpu.SemaphoreType.DMA((n_slots,)),
                        pltpu.VMEM((tt, d), F32)],
        compiler_params=_cparams(("arbitrary", "arbitrary")),
        name="peer_experts",
    )(idx3, gate.reshape(b, s, LANES), h2, x1, g2, normf_g.reshape(1, d), uv)


def _layer_tail(x, mods, attn, h_m, o_m, attn_g, mlstm_g, w_out, norm2_g, w_query, sub_keys, uv, normf_g, tm, tt):
    g1, sh2, sc2, g2 = mods
    b, s, d = x.shape
    x1, h2, pq = _back(x, attn, h_m, o_m, g1, sc2, sh2, attn_g, mlstm_g, w_out, norm2_g, w_query, tm)
    idx, gate = _peer_route(pq.reshape(b * s, -1), sub_keys)
    return _peer_experts(idx, gate, h2, x1, g2, normf_g, uv, tt)


def _prompt_path(x, mod6, p):
    b, s, d = x.shape
    sh1, sc1, g1, sh2, sc2, g2 = [m.reshape(b, 1, d) for m in mod6]
    tm = min(512, s)
    q, k, v, qb, kb, vb, qk_m, v_m, o_m, if_m, if_t = _front(x, sc1, sh1, p["norm1_g"], p["w_in"], p["b_if"], tm)
    attn = _moba_prompt(q, k, qb, kb, vb)
    h_m, c_p, n_p, m_p = _mlstm_prompt(qk_m, v_m, if_m, if_t, p["conv_w"], min(256, s))
    y = _layer_tail(x, (g1, sh2, sc2, g2), attn, h_m, o_m, p["attn_norm_g"], p["mlstm_norm_g"], p["w_out"],
                    p["norm2_g"], p["peer_w_query"], p["peer_sub_keys"], p["uv"], p["normf_g"],
                    min(256, s), min(64, s))
    conv_p = qk_m[:, s - (CONV_W - 1):, :]
    return y, k, v, c_p, n_p, m_p[:, 0, :MLSTM_HEADS], conv_p


PAGES_PER_STEP = 8


def _page_sum_body(pt_ref, *refs, n_in, ppb):
    del pt_ref
    g = pl.program_id(1)
    out_ref = refs[n_in]

    @pl.when(g == 0)
    def _():
        out_ref[...] = jnp.zeros_like(out_ref)

    lane = _iota((D_ATTN, LANES), 1)
    acc = out_ref[0]
    for j in range(0, n_in, ppb):
        pages = refs[j][0, 0]
        for jj in range(1, ppb):
            pages = pages + refs[j + jj][0, 0]
        block_sum = jnp.sum(pages.reshape(D_ATTN, pages.shape[-1]), axis=-1, keepdims=True)
        acc = jnp.where(lane == (g * n_in + j) // ppb, block_sum, acc)
    out_ref[0] = acc


def _page_sums(kp5, pt_flat, layer, db, n_pages, ppb):
    assert n_pages <= LANES
    n_in = min(PAGES_PER_STEP, n_pages)
    assert n_pages % n_in == 0 and n_in % ppb == 0
    ps = kp5.shape[-1]

    def page_spec(j):
        return pl.BlockSpec((1, 1, ATTN_HEADS, ATTN_HEAD_DIM, ps),
                            lambda d, g, pt: (layer, pt[d * n_pages + g * n_in + j], 0, 0, 0))

    return pl.pallas_call(
        functools.partial(_page_sum_body, n_in=n_in, ppb=ppb),
        grid_spec=pltpu.PrefetchScalarGridSpec(
            num_scalar_prefetch=1,
            grid=(db, n_pages // n_in),
            in_specs=[page_spec(j) for j in range(n_in)],
            out_specs=pl.BlockSpec((1, D_ATTN, LANES), lambda d, g, pt: (d, 0, 0)),
        ),
        out_shape=jax.ShapeDtypeStruct((db, D_ATTN, LANES), F32),
        compiler_params=_cparams(("arbitrary", "arbitrary")),
        name="sample_page_sums",
    )(pt_flat, *([kp5] * n_in))


def _sample_select_body(q_ref, ks_ref, o_ref, *, nbp):
    q = jnp.broadcast_to(q_ref[0], (ATTN_HEADS, D_ATTN))
    qbd = jnp.where(_iota(q.shape, 1) // ATTN_HEAD_DIM == _iota(q.shape, 0), q, 0.0)
    s = jnp.dot(qbd, ks_ref[0], precision=HI, preferred_element_type=F32) * (1.0 / MOBA_BLOCK)
    lane = _iota(s.shape, 1)
    s = jnp.where(lane < nbp, s, -jnp.inf)
    top = jnp.zeros(s.shape, I32)
    for r in range(min(MOBA_TOPK, nbp)):
        m = jnp.max(s, axis=-1, keepdims=True)
        i = jnp.min(jnp.where(s == m, lane, LANES), axis=-1, keepdims=True)
        top = jnp.where(lane == r, i, top)
        s = jnp.where(lane == i, -jnp.inf, s)
    o_ref[0] = top


def _sample_select(q, ksum, nbp):
    db = q.shape[0]
    return pl.pallas_call(
        functools.partial(_sample_select_body, nbp=nbp),
        grid=(db,),
        in_specs=[pl.BlockSpec((1, 1, D_ATTN), lambda d: (d, 0, 0)),
                  pl.BlockSpec((1, D_ATTN, LANES), lambda d: (d, 0, 0))],
        out_specs=pl.BlockSpec((1, ATTN_HEADS, LANES), lambda d: (d, 0, 0)),
        out_shape=jax.ShapeDtypeStruct((db, ATTN_HEADS, LANES), I32),
        compiler_params=_cparams(("arbitrary",)),
        name="sample_select",
    )(q.reshape(db, 1, D_ATTN), ksum)


def _row_to_col(row):
    eye = _iota((LANES, LANES), 0) == _iota((LANES, LANES), 1)
    return jnp.sum(jnp.where(eye, jnp.broadcast_to(row, (LANES, LANES)), 0.0), axis=-1, keepdims=True)


def _col_to_row(col):
    eye = _iota((LANES, LANES), 0) == _iota((LANES, LANES), 1)
    return jnp.sum(jnp.where(eye, jnp.broadcast_to(col, (LANES, LANES)), 0.0), axis=0, keepdims=True)


def _sample_attend_body(pt_ref, top_ref, q_ref, kn_ref, vn_ref, kp_ref, vp_ref, o_ref, kbuf, vbuf, sem,
                        *, layer, n_pages, n_sel, ppb):
    d = pl.program_id(0)
    per_head = n_sel * ppb

    def copies(h, r, j):
        blk = top_ref[(d * ATTN_HEADS + h) * n_sel + r]
        phys = pt_ref[d * n_pages + blk * ppb + j]
        slot = h * per_head + r * ppb + j
        return (pltpu.make_async_copy(kp_ref.at[layer, phys, h], kbuf.at[slot], sem.at[0]),
                pltpu.make_async_copy(vp_ref.at[layer, phys, h], vbuf.at[slot], sem.at[1]))

    todo = [(h, r, j) for h in range(ATTN_HEADS) for r in range(n_sel) for j in range(ppb)]
    for hrj in todo:
        for c in copies(*hrj):
            c.start()
    q_row = q_ref[0] * (ATTN_HEAD_DIM ** -0.5)
    k_row = kn_ref[0]
    v_row = vn_ref[0]
    tiles = D_ATTN // LANES
    q_col = jnp.concatenate([_row_to_col(q_row[:, t * LANES:(t + 1) * LANES]) for t in range(tiles)], axis=0)
    head_of_lane = _iota((1, D_ATTN), 1) // ATTN_HEAD_DIM
    qk_own = q_row * k_row
    for hrj in todo:
        for c in copies(*hrj):
            c.wait()
    acc_cols = []
    p_own_row = jnp.zeros((1, D_ATTN), F32)
    l_row = jnp.ones((1, D_ATTN), F32)
    for h in range(ATTN_HEADS):
        qc = q_col[h * ATTN_HEAD_DIM:(h + 1) * ATTN_HEAD_DIM]
        base = h * per_head
        s_pages = [jnp.sum(kbuf[base + g] * qc, axis=0, keepdims=True) for g in range(per_head)]
        s_own = jnp.sum(jnp.where(head_of_lane == h, qk_own, 0.0), axis=-1, keepdims=True)
        m = s_own
        for s in s_pages:
            m = jnp.maximum(m, jnp.max(s, axis=-1, keepdims=True))
        p_own = jnp.exp(s_own - m)
        l = p_own
        acc = jnp.zeros(kbuf.shape[1:], F32)
        for g, s in enumerate(s_pages):
            p = jnp.exp(s - m)
            l = l + jnp.sum(p, axis=-1, keepdims=True)
            acc = acc + vbuf[base + g] * p
        acc_cols.append(jnp.sum(acc, axis=-1, keepdims=True))
        p_own_row = jnp.where(head_of_lane == h, p_own, p_own_row)
        l_row = jnp.where(head_of_lane == h, l, l_row)
    acc_col = jnp.concatenate(acc_cols, axis=0)
    acc_row = jnp.concatenate([_col_to_row(acc_col[t * LANES:(t + 1) * LANES]) for t in range(tiles)], axis=1)
    o_ref[0] = (acc_row + p_own_row * v_row) / l_row


def _sample_attend(q, k_new, v_new, kp5, vp5, pt_flat, top_flat, layer, n_pages, n_sel, ppb):
    db = q.shape[0]
    ps = kp5.shape[-1]
    row = pl.BlockSpec((1, 1, D_ATTN), lambda d, pt, top: (d, 0, 0))
    n_buf = ATTN_HEADS * n_sel * ppb
    out = pl.pallas_call(
        functools.partial(_sample_attend_body, layer=layer, n_pages=n_pages, n_sel=n_sel, ppb=ppb),
        grid_spec=pltpu.PrefetchScalarGridSpec(
            num_scalar_prefetch=2,
            grid=(db,),
            in_specs=[row, row, row, pl.BlockSpec(memory_space=pl.ANY), pl.BlockSpec(memory_space=pl.ANY)],
            out_specs=row,
            scratch_shapes=[pltpu.VMEM((n_buf, ATTN_HEAD_DIM, ps), F32), pltpu.VMEM((n_buf, ATTN_HEAD_DIM, ps), F32),
                            pltpu.SemaphoreType.DMA((2,))],
        ),
        out_shape=jax.ShapeDtypeStruct((db, 1, D_ATTN), F32),
        compiler_params=_cparams(("arbitrary",)),
        name="sample_attend",
    )(pt_flat, top_flat, q.reshape(db, 1, D_ATTN), k_new.reshape(db, 1, D_ATTN), v_new.reshape(db, 1, D_ATTN),
      kp5, vp5)
    return out.reshape(db, D_ATTN)


def _moba_sample(q, k_new, v_new, k_pool, v_pool, page_table, layer):
    db = q.shape[0]
    ps = k_pool.shape[2]
    n_pages = page_table.shape[1]
    ppb = MOBA_BLOCK // ps
    nbp = (n_pages * ps) // MOBA_BLOCK
    assert nbp >= 1 and (n_pages - nbp * ppb) * ps == 0, "the cached rows must fill whole MoBA blocks"
    n_sel = min(MOBA_TOPK, nbp)
    kp5 = jnp.transpose(k_pool, (0, 1, 3, 4, 2))
    vp5 = jnp.transpose(v_pool, (0, 1, 3, 4, 2))
    pt_flat = page_table.reshape(-1)
    ksum = _page_sums(kp5, pt_flat, layer, db, n_pages, ppb)
    top = _sample_select(q, ksum, nbp)
    top_flat = top[:, :, :n_sel].reshape(-1)
    return _sample_attend(q, k_new, v_new, kp5, vp5, pt_flat, top_flat, layer, n_pages, n_sel, ppb)


def _mlstm_step_body(qk_ref, vm_ref, if_ref, conv_ref, cw_ref, c_ref, n_ref, m_ref,
                     h_ref, c_out, n_out, m_out):
    tb = qk_ref.shape[0]
    w = cw_ref[...]
    lane = _iota((1, LANES), 1)

    def one(r):
        x = qk_ref[pl.ds(r, 1), :]
        cb = conv_ref[pl.ds(r, 1)][0]
        y = cb[0:1] * w[0:1]
        for j in range(1, CONV_W - 1):
            y = y + cb[j:j + 1] * w[j:j + 1]
        y = y + x * w[CONV_W - 1:CONV_W]
        qkc = _silu(y)
        gates = if_ref[pl.ds(r, 1), :]
        m_all = m_ref[pl.ds(r, 1), :]
        v_all = vm_ref[pl.ds(r, 1), :]
        for h in range(MLSTM_HEADS):
            sl = slice(h * MLSTM_HEAD_DIM, (h + 1) * MLSTM_HEAD_DIM)
            q = qkc[:, sl]
            k = qkc[:, D_MLSTM + h * MLSTM_HEAD_DIM:D_MLSTM + (h + 1) * MLSTM_HEAD_DIM] * (MLSTM_HEAD_DIM ** -0.5)
            v = v_all[:, sl]
            ig = gates[:, h:h + 1]
            fg = gates[:, MLSTM_HEADS + h:MLSTM_HEADS + h + 1]
            m_prev = m_all[:, h:h + 1]
            inter = _log_sigmoid(fg) + m_prev
            m_t = jnp.maximum(inter, ig)
            wgt = jnp.exp(ig - m_t)
            w0 = jnp.exp(inter - m_t)
            c_prev = c_ref[pl.ds(r, 1), h][0]
            n_prev = n_ref[pl.ds(r, 1), h]
            sqk = jnp.sum(q * k, axis=-1, keepdims=True) * wgt
            cq = _nt(jnp.broadcast_to(q, (SUBLANES, MLSTM_HEAD_DIM)), c_prev, precision=HI)[0:1]
            num = w0 * cq + sqk * v
            den = w0 * jnp.sum(n_prev * q, axis=-1, keepdims=True) + sqk
            h_ref[pl.ds(r, 1), sl] = num / jnp.maximum(jnp.abs(den), jnp.exp(-m_t))
            c_out[pl.ds(r, 1), h] = (w0 * c_prev + (wgt * _row_to_col(v)) * k)[None]
            n_out[pl.ds(r, 1), h] = w0 * n_prev + wgt * k
            m_all = jnp.where(lane == h, m_t, m_all)
        m_out[pl.ds(r, 1), :] = m_all

    for r in range(tb):
        one(r)


def _mlstm_step(qk_m, v_m, if_m, conv_state, conv_w, c, n, m_pad, tb):
    db = qk_m.shape[0]
    row = lambda w: pl.BlockSpec((tb, w), lambda i: (i, 0))
    c_spec = pl.BlockSpec((tb, MLSTM_HEADS, MLSTM_HEAD_DIM, MLSTM_HEAD_DIM), lambda i: (i, 0, 0, 0))
    n_spec = pl.BlockSpec((tb, MLSTM_HEADS, MLSTM_HEAD_DIM), lambda i: (i, 0, 0))
    return pl.pallas_call(
        _mlstm_step_body,
        grid=(db // tb,),
        in_specs=[row(2 * D_MLSTM), row(D_MLSTM), row(LANES),
                  pl.BlockSpec((tb, CONV_W - 1, 2 * D_MLSTM), lambda i: (i, 0, 0)),
                  pl.BlockSpec((CONV_W, 2 * D_MLSTM), lambda i: (0, 0)), c_spec, n_spec, row(LANES)],
        out_specs=[row(D_MLSTM), c_spec, n_spec, row(LANES)],
        out_shape=[jax.ShapeDtypeStruct((db, D_MLSTM), F32), jax.ShapeDtypeStruct(c.shape, F32),
                   jax.ShapeDtypeStruct(n.shape, F32), jax.ShapeDtypeStruct((db, LANES), F32)],
        compiler_params=_cparams(("arbitrary",)),
        name="mlstm_step",
    )(qk_m, v_m, if_m, conv_state, conv_w, c, n, m_pad)


def _sample_path(x, mod6, p, k_pool, v_pool, page_table, c0, n0, m0, conv0, layer):
    db, t, d = x.shape
    assert t == 1, "one new token per sequence"
    assert db % SUBLANES == 0
    xs = x.reshape(1, db, d)
    sh1, sc1, g1, sh2, sc2, g2 = [m.reshape(1, db, d) for m in mod6]
    q, k, v, _, _, _, qk_m, v_m, o_m, if_m, _ = _front(xs, sc1, sh1, p["norm1_g"], p["w_in"], p["b_if"], db)
    q, k, v, qk_m, v_m, if_m = [a[0] for a in (q, k, v, qk_m, v_m, if_m)]
    attn = _moba_sample(q, k, v, k_pool, v_pool, page_table, layer)
    m_pad = jnp.pad(m0, ((0, 0), (0, LANES - MLSTM_HEADS)))
    h_s, c_s, n_s, m_s = _mlstm_step(qk_m, v_m, if_m, conv0, p["conv_w"], c0, n0, m_pad, SUBLANES)
    y = _layer_tail(xs, (g1, sh2, sc2, g2), attn[None], h_s[None], o_m, p["attn_norm_g"], p["mlstm_norm_g"],
                    p["w_out"], p["norm2_g"], p["peer_w_query"], p["peer_sub_keys"], p["uv"], p["normf_g"],
                    db, min(64, db))
    conv_s = jnp.concatenate([conv0[:, 1:], qk_m[:, None, :]], axis=1)
    return y.reshape(db, t, d), k, v, c_s, n_s, m_s[:, :MLSTM_HEADS], conv_s


def kernel(x_prompt, x_sample, cache_k_pool, cache_v_pool, page_table, state_mlstm_C, state_mlstm_n, state_mlstm_m, state_conv, c_prompt, c_sample, w_ada, b_ada, norm1_g, w_in, b_if, conv_w, attn_norm_g, mlstm_norm_g, w_out, norm2_g, peer_w_query, peer_sub_keys, peer_u, peer_v, normf_g):
    b, s, d = x_prompt.shape
    db = x_sample.shape[0]
    l = 0
    p = dict(norm1_g=norm1_g[l], w_in=w_in[l], b_if=b_if[l], conv_w=conv_w[l], attn_norm_g=attn_norm_g[l],
             mlstm_norm_g=mlstm_norm_g[l], w_out=w_out[l], norm2_g=norm2_g[l], peer_w_query=peer_w_query[l],
             peer_sub_keys=peer_sub_keys[l], normf_g=normf_g,
             uv=jnp.concatenate([_pack_bf16_pairs(peer_u[l]), _pack_bf16_pairs(peer_v[l])], axis=1)[:, None, :])
    rows = b + db
    pad = -rows % SUBLANES
    c_all = jnp.pad(jnp.concatenate([c_prompt, c_sample], axis=0), ((0, pad), (0, 0)))
    mod = _ada(c_all, w_ada[l], b_ada[l])
    mod_p = [mod[:b, i * d:(i + 1) * d] for i in range(6)]
    mod_s = [mod[b:rows, i * d:(i + 1) * d] for i in range(6)]
    y_p, k_p, v_p, c_p, n_p, m_p, conv_p = _prompt_path(x_prompt, mod_p, p)
    y_s, k_s, v_s, c_s, n_s, m_s, conv_s = _sample_path(
        x_sample, mod_s, p, cache_k_pool, cache_v_pool, page_table,
        state_mlstm_C[l], state_mlstm_n[l], state_mlstm_m[l], state_conv[l], l)
    hp = (b, s, ATTN_HEADS, ATTN_HEAD_DIM)
    hs = (db, 1, ATTN_HEADS, ATTN_HEAD_DIM)
    return (y_p, y_s, k_p.reshape(hp)[None], v_p.reshape(hp)[None], k_s.reshape(hs)[None], v_s.reshape(hs)[None],
            c_p[None], n_p[None], m_p[None], conv_p[None], c_s[None], n_s[None], m_s[None], conv_s[None])
```

```python
import functools

import jax
import jax.numpy as jnp
from jax import lax
from jax.experimental import pallas as pl
from jax.experimental.pallas import tpu as pltpu

F32 = jnp.float32
BF16 = jnp.bfloat16
I32 = jnp.int32
HI = lax.Precision.HIGHEST

SUBLANES = 8
LANES = 128
VMEM_LIMIT_BYTES = 56 * 1024 * 1024

D_MODEL = 1024
D_ATTN = 512
D_MLSTM = 512
ATTN_HEADS = 8
ATTN_HEAD_DIM = 64
MLSTM_HEADS = 4
MLSTM_HEAD_DIM = 128
MOBA_BLOCK = 256
MOBA_TOPK = 3
CONV_W = 4
PEER_HEADS = 8
PEER_N_KEYS = 128
PEER_TOPK = 16
PEER_HALF = 128
EPS = 1e-6
N_GATES = 2 * MLSTM_HEADS
C_Q, C_K, C_V, C_QK, C_VM, C_OM, C_IF = 0, 512, 1024, 1536, 2560, 3072, 3584
NEG = -1e30


def _cparams(sem):
    return pltpu.CompilerParams(dimension_semantics=sem, vmem_limit_bytes=VMEM_LIMIT_BYTES)


def _nt(a, b, **kw):
    return lax.dot_general(a, b, (((1,), (1,)), ((), ())), preferred_element_type=F32, **kw)


def _tn(a, b, **kw):
    return lax.dot_general(a, b, (((0,), (0,)), ((), ())), preferred_element_type=F32, **kw)


def _silu(x):
    return x * jax.nn.sigmoid(x)


def _log_sigmoid(x):
    return jnp.minimum(x, 0.0) - jnp.log1p(jnp.exp(-jnp.abs(x)))


def _iota(shape, dim):
    return lax.broadcasted_iota(I32, shape, dim)


def _ada_body(c_ref, w_ref, b_ref, o_ref):
    o_ref[...] = jnp.dot(_silu(c_ref[...]), w_ref[...], precision=HI, preferred_element_type=F32) + b_ref[...]


def _ada(c, w_ada, b_ada):
    rows, d = c.shape
    n = w_ada.shape[1]
    tn = 512
    return pl.pallas_call(
        _ada_body,
        grid=(n // tn,),
        in_specs=[pl.BlockSpec((rows, d), lambda j: (0, 0)),
                  pl.BlockSpec((d, tn), lambda j: (0, j)),
                  pl.BlockSpec((1, tn), lambda j: (0, j))],
        out_specs=pl.BlockSpec((rows, tn), lambda j: (0, j)),
        out_shape=jax.ShapeDtypeStruct((rows, n), F32),
        compiler_params=_cparams(("arbitrary",)),
        name="ada_mod",
    )(c, w_ada, b_ada.reshape(1, n))


def _front_body(x_ref, sc_ref, sh_ref, g_ref, w_ref, wif_ref, wift_ref, bif_ref, bift_ref,
                q_ref, k_ref, v_ref, qb_ref, kb_ref, vb_ref, qk_ref, vm_ref, om_ref, if_ref, ift_ref):
    x = x_ref[0]
    h = x * lax.rsqrt(jnp.mean(x * x, axis=-1, keepdims=True) + EPS) * g_ref[...]
    h = h * (1.0 + sc_ref[0]) + sh_ref[0]
    hb = h.astype(BF16)

    def seg(lo, hi):
        return jnp.dot(hb, w_ref[:, lo:hi], preferred_element_type=F32)

    q = seg(C_Q, C_K)
    q_ref[0] = q
    qb_ref[0] = (q * (ATTN_HEAD_DIM ** -0.5)).astype(BF16)
    k = seg(C_K, C_V)
    k_ref[0] = k
    kb_ref[0] = k.astype(BF16)
    v = seg(C_V, C_QK)
    v_ref[0] = v
    vb_ref[0] = v.astype(BF16)
    qk_ref[0] = seg(C_QK, C_VM)
    vm_ref[0] = seg(C_VM, C_OM)
    om_ref[0] = seg(C_OM, C_IF)
    if_ref[0] = jnp.dot(h, wif_ref[...], precision=HI, preferred_element_type=F32) + bif_ref[...]
    ift_ref[0] = _nt(wift_ref[...], h, precision=HI) + bift_ref[...]


def _front(x, sc, sh, norm_g, w_in, b_if, tm):
    b, s, d = x.shape
    per_row = sc.shape[1] != 1
    mod_spec = (pl.BlockSpec((1, tm, d), lambda bi, i: (bi, i, 0)) if per_row
                else pl.BlockSpec((1, 1, d), lambda bi, i: (bi, 0, 0)))
    w_main = w_in[:, :C_IF].astype(BF16)
    w_if = jnp.pad(w_in[:, C_IF:], ((0, 0), (0, LANES - N_GATES)))
    w_ift = w_in[:, C_IF:].T
    b_row = jnp.pad(b_if, (0, LANES - N_GATES)).reshape(1, LANES)
    b_col = b_if.reshape(N_GATES, 1)
    const = lambda shape: pl.BlockSpec(shape, lambda bi, i: (0,) * len(shape))
    tok = lambda w: pl.BlockSpec((1, tm, w), lambda bi, i: (bi, i, 0))
    out_shapes = [
        jax.ShapeDtypeStruct((b, s, D_ATTN), F32),
        jax.ShapeDtypeStruct((b, s, D_ATTN), F32),
        jax.ShapeDtypeStruct((b, s, D_ATTN), F32),
        jax.ShapeDtypeStruct((b, s, D_ATTN), BF16),
        jax.ShapeDtypeStruct((b, s, D_ATTN), BF16),
        jax.ShapeDtypeStruct((b, s, D_ATTN), BF16),
        jax.ShapeDtypeStruct((b, s, 2 * D_MLSTM), F32),
        jax.ShapeDtypeStruct((b, s, D_MLSTM), F32),
        jax.ShapeDtypeStruct((b, s, D_MLSTM), F32),
        jax.ShapeDtypeStruct((b, s, LANES), F32),
        jax.ShapeDtypeStruct((b, N_GATES, s), F32),
    ]
    out_specs = [tok(D_ATTN)] * 6 + [tok(2 * D_MLSTM), tok(D_MLSTM), tok(D_MLSTM), tok(LANES),
                                     pl.BlockSpec((1, N_GATES, tm), lambda bi, i: (bi, 0, i))]
    return pl.pallas_call(
        _front_body,
        grid=(b, s // tm),
        in_specs=[tok(d), mod_spec, mod_spec, const((1, d)), const((d, C_IF)), const((d, LANES)),
                  const((N_GATES, d)), const((1, LANES)), const((N_GATES, 1))],
        out_specs=out_specs,
        out_shape=out_shapes,
        compiler_params=_cparams(("arbitrary", "arbitrary")),
        name="front",
    )(x, sc, sh, norm_g.reshape(1, d), w_main, w_if, w_ift, b_row, b_col)


def _km_body(k_ref, km_ref, *, hb):
    n = pl.program_id(1)

    @pl.when(n == 0)
    def _():
        km_ref[...] = jnp.zeros_like(km_ref)

    lw = km_ref.shape[2]
    ones = jnp.full((MOBA_BLOCK, lw), 1.0 / MOBA_BLOCK, F32)
    mean_b = _tn(k_ref[0], ones, precision=HI)
    row = _iota((D_ATTN, lw), 0)
    col = _iota((D_ATTN, lw), 1)
    km_ref[0] += jnp.where(col == (row // ATTN_HEAD_DIM) * hb + n, mean_b, 0.0)


def _block_means(k, nb, hb):
    lw = ATTN_HEADS * hb
    b = k.shape[0]
    return pl.pallas_call(
        functools.partial(_km_body, hb=hb),
        grid=(b, nb),
        in_specs=[pl.BlockSpec((1, MOBA_BLOCK, D_ATTN), lambda bi, n: (bi, n, 0))],
        out_specs=pl.BlockSpec((1, D_ATTN, lw), lambda bi, n: (bi, 0, 0)),
        out_shape=jax.ShapeDtypeStruct((b, D_ATTN, lw), F32),
        compiler_params=_cparams(("arbitrary", "arbitrary")),
        name="moba_block_means",
    )(k)


def _sel_body(q_ref, km_ref, o_ref, *, hb, n_top):
    own = pl.program_id(1)
    s = jnp.dot(q_ref[0], km_ref[0], precision=HI, preferred_element_type=F32)
    shape = s.shape
    lw = shape[1]
    lane = _iota(shape, 1)
    blk = lane & (hb - 1)
    head = lane >> (hb.bit_length() - 1)
    valid = (blk < own) & (head < ATTN_HEADS)
    sel = jnp.zeros(shape, jnp.bool_)
    for h in range(ATTN_HEADS):
        sm = jnp.where(valid & (head == h), s, -jnp.inf)
        for _ in range(n_top):
            m = jnp.max(sm, axis=-1, keepdims=True)
            i = jnp.min(jnp.where(sm == m, lane, lw), axis=-1, keepdims=True)
            hit = lane == i
            sel = sel | (hit & (m > -jnp.inf))
            sm = jnp.where(hit, -jnp.inf, sm)
    o_ref[0] = jnp.where(sel, 0.0, NEG)


def _moba_select(q, km, nb, hb):
    b, s, _ = q.shape
    lw = km.shape[2]
    return pl.pallas_call(
        functools.partial(_sel_body, hb=hb, n_top=min(MOBA_TOPK, nb)),
        grid=(b, s // MOBA_BLOCK),
        in_specs=[pl.BlockSpec((1, MOBA_BLOCK, D_ATTN), lambda bi, i: (bi, i, 0)),
                  pl.BlockSpec((1, D_ATTN, lw), lambda bi, i: (bi, 0, 0))],
        out_specs=pl.BlockSpec((1, MOBA_BLOCK, lw), lambda bi, i: (bi, i, 0)),
        out_shape=jax.ShapeDtypeStruct((b, s, lw), F32),
        compiler_params=_cparams(("arbitrary", "arbitrary")),
        name="moba_select",
    )(q, km)


def _attn_body(q_ref, k_ref, v_ref, sb_ref, lid_ref, lidm_ref, o_ref, *, hb):
    own = pl.program_id(2)
    tq = MOBA_BLOCK
    tiles = q_ref.shape[2] // LANES
    n_heads = 2 * tiles
    lane = _iota((tq, LANES), 1)
    low = lane < ATTN_HEAD_DIM
    sb = sb_ref[0]
    one = jnp.ones((2 * tq, LANES), BF16)
    zero = jnp.zeros((2 * tq, LANES), BF16)
    lane_b = lid_ref[...]
    lane_m = lidm_ref[...]

    def own_lanes(is_low, rows):
        ids = lane_b[:rows]
        return ids < ATTN_HEAD_DIM if is_low else ids >= ATTN_HEAD_DIM

    qs, mine, bias0 = [], [], []
    for hh in range(n_heads):
        t = hh // 2
        q = q_ref[0, :, t * LANES:(t + 1) * LANES]
        is_low = hh % 2 == 0
        first = ATTN_HEAD_DIM if is_low else 0
        tile, off = divmod(hh * hb, LANES)
        bias = sb[:, tile * LANES:(tile + 1) * LANES]
        shift = (first - off) % LANES
        if shift:
            bias = pltpu.roll(bias, shift, 1)
        qs.append(jnp.where(own_lanes(is_low, tq), q, bias.astype(BF16)))
        mine.append(is_low)
        bias0.append(first)
    row = _iota((tq, tq), 0)
    col = _iota((tq, tq), 1)

    def attend(j, state, is_own, n_blk):
        rows = n_blk * tq
        start = pl.multiple_of(j * tq, rows)
        new = []
        for hh in range(n_heads):
            sl = slice((hh // 2) * LANES, (hh // 2 + 1) * LANES)
            kj = k_ref[0, pl.ds(start, rows), sl]
            vj = v_ref[0, pl.ds(start, rows), sl]
            m, acc = state[hh]
            keep = own_lanes(mine[hh], rows)
            if is_own:
                s = _nt(qs[hh], jnp.where(keep, kj, zero[:rows]))
                s = jnp.where(col <= row, s, -jnp.inf)
            else:
                hot = lane_m[:rows] == (bias0[hh] + j).astype(F32).astype(BF16)
                s = _nt(qs[hh], jnp.where(keep, kj, jnp.where(hot, one[:rows], zero[:rows])))
            m_new = jnp.maximum(m, jnp.max(s, axis=-1, keepdims=True))
            p = jnp.exp(s - m_new)
            acc = jnp.exp(m - m_new) * acc + jnp.dot(p.astype(BF16), jnp.where(keep, vj, one[:rows]),
                                                      preferred_element_type=F32)
            new.append((m_new, acc))
        return tuple(new)

    init = tuple((jnp.full((tq, 1), -jnp.inf, F32), jnp.zeros((tq, LANES), F32)) for _ in range(n_heads))
    state = attend(own, init, True, 1)
    state = lax.fori_loop(0, own // 2, lambda i, st: attend(2 * i, st, False, 2), state)
    state = lax.cond(own % 2 == 1, lambda st: attend(own - 1, st, False, 1), lambda st: st, state)
    for t in range(tiles):
        outs = []
        for hh in (2 * t, 2 * t + 1):
            acc = state[hh][1]
            outs.append(acc / pltpu.roll(acc, ATTN_HEAD_DIM, 1))
        o_ref[0, :, t * LANES:(t + 1) * LANES] = jnp.where(low, outs[0], outs[1])


ATTN_LANES_PER_STEP = 2 * LANES


def _moba_attend(qb, kb, vb, selbias, hb):
    b, s, _ = qb.shape
    gw = ATTN_LANES_PER_STEP
    groups = D_ATTN // gw
    sw = selbias.shape[2] // groups
    assert sw % LANES == 0
    lane_id = jnp.broadcast_to(jnp.arange(LANES, dtype=F32), (2 * MOBA_BLOCK, LANES))
    second_block = (jnp.arange(2 * MOBA_BLOCK, dtype=F32) >= MOBA_BLOCK).astype(F32)[:, None]
    return pl.pallas_call(
        functools.partial(_attn_body, hb=hb),
        grid=(b, groups, s // MOBA_BLOCK),
        in_specs=[pl.BlockSpec((1, MOBA_BLOCK, gw), lambda bi, p, i: (bi, i, p)),
                  pl.BlockSpec((1, s, gw), lambda bi, p, i: (bi, 0, p)),
                  pl.BlockSpec((1, s, gw), lambda bi, p, i: (bi, 0, p)),
                  pl.BlockSpec((1, MOBA_BLOCK, sw), lambda bi, p, i: (bi, i, p)),
                  pl.BlockSpec((2 * MOBA_BLOCK, LANES), lambda bi, p, i: (0, 0)),
                  pl.BlockSpec((2 * MOBA_BLOCK, LANES), lambda bi, p, i: (0, 0))],
        out_specs=pl.BlockSpec((1, MOBA_BLOCK, gw), lambda bi, p, i: (bi, i, p)),
        out_shape=jax.ShapeDtypeStruct((b, s, D_ATTN), F32),
        compiler_params=_cparams(("arbitrary", "arbitrary", "arbitrary")),
        name="moba_attend",
    )(qb, kb, vb, selbias, lane_id.astype(BF16), (lane_id - second_block).astype(BF16))


MIN_HEAD_STRIDE = 32


def _moba_prompt(q, k, qb, kb, vb):
    s = q.shape[1]
    assert s % MOBA_BLOCK == 0
    nb = s // MOBA_BLOCK
    assert nb & (nb - 1) == 0, "block count must be a power of two"
    hb = max(nb, MIN_HEAD_STRIDE)
    assert hb <= ATTN_HEAD_DIM, "the per-block bias must fit in the free half of a lane tile"
    km = _block_means(k, nb, hb)
    selbias = _moba_select(q, km, nb, hb)
    return _moba_attend(qb, kb, vb, selbias, hb)


def _mlstm_body(qk_ref, vm_ref, if_ref, ift_ref, cw_ref, h_ref, c_ref, n_ref, m_ref, prev_scr):
    i = pl.program_id(1)
    L = qk_ref.shape[1]

    @pl.when(i == 0)
    def _():
        prev_scr[...] = jnp.zeros_like(prev_scr)
        c_ref[...] = jnp.zeros_like(c_ref)
        n_ref[...] = jnp.zeros_like(n_ref)
        m_ref[...] = jnp.zeros_like(m_ref)

    x = qk_ref[0]
    prev = prev_scr[...]
    w = cw_ref[...]
    rows = _iota(x.shape, 0)
    y = x * w[CONV_W - 1:CONV_W]
    for sft in range(1, CONV_W):
        xs = jnp.where(rows >= sft, pltpu.roll(x, sft, 0), pltpu.roll(prev, sft, 0))
        y = y + xs * w[CONV_W - 1 - sft:CONV_W - sft]
    prev_scr[...] = x
    qkc = _silu(y)

    gcol = if_ref[0]
    grow = ift_ref[0]
    tri_r = _iota((L, L), 0)
    tri_c = _iota((L, L), 1)
    lower = tri_c <= tri_r
    b_col = jnp.dot(lower.astype(F32), _log_sigmoid(gcol), precision=HI, preferred_element_type=F32)
    b_row = jnp.dot(_log_sigmoid(grow), (tri_r <= tri_c).astype(F32), precision=HI, preferred_element_type=F32)
    m_all = m_ref[0]
    lane = _iota(m_all.shape, 1)
    for h in range(MLSTM_HEADS):
        sl = slice(h * MLSTM_HEAD_DIM, (h + 1) * MLSTM_HEAD_DIM)
        qh = qkc[:, sl]
        kh = qkc[:, D_MLSTM + h * MLSTM_HEAD_DIM:D_MLSTM + (h + 1) * MLSTM_HEAD_DIM] * (MLSTM_HEAD_DIM ** -0.5)
        vh = vm_ref[0][:, sl]
        fh = MLSTM_HEADS + h
        bc = b_col[:, fh:fh + 1]
        br = b_row[fh:fh + 1, :]
        ic = gcol[:, h:h + 1]
        ir = grow[h:h + 1, :]
        m_prev = m_all[:, h:h + 1]
        c_prev = c_ref[0, h]
        n_prev = n_ref[0, h:h + 1, :]
        dmat = jnp.where(lower, bc - br + ir, -jnp.inf)
        inter = bc + m_prev
        m_t = jnp.maximum(inter, jnp.max(dmat, axis=-1, keepdims=True))
        wmat = jnp.exp(dmat - m_t)
        w0 = jnp.exp(inter - m_t)
        qb, kb, vb = qh.astype(BF16), kh.astype(BF16), vh.astype(BF16)
        sqk = _nt(qb, kb) * wmat
        num = w0 * _nt(qb, c_prev.astype(BF16)) + jnp.dot(sqk.astype(BF16), vb, preferred_element_type=F32)
        den = w0 * jnp.sum(qh * n_prev, axis=-1, keepdims=True) + jnp.sum(sqk, axis=-1, keepdims=True)
        h_ref[0, :, sl] = num / jnp.maximum(jnp.abs(den), jnp.exp(-m_t))
        m_new = m_t[L - 1:L, :]
        b_last = bc[L - 1:L, :]
        g = jnp.exp(b_last - bc + ic - m_new)
        decay = jnp.exp(b_last + m_prev - m_new)
        c_ref[0, h] = decay * c_prev + _tn((g * vh).astype(BF16), kb)
        n_ref[0, h:h + 1, :] = decay * n_prev + jnp.sum(g * kh, axis=0, keepdims=True)
        m_all = jnp.where(lane == h, m_new, m_all)
    m_ref[0] = m_all


def _mlstm_prompt(qk_m, v_m, if_m, if_t, conv_w, chunk):
    b, s, _ = qk_m.shape
    tok = lambda w: pl.BlockSpec((1, chunk, w), lambda bi, i: (bi, i, 0))
    return pl.pallas_call(
        _mlstm_body,
        grid=(b, s // chunk),
        in_specs=[tok(2 * D_MLSTM), tok(D_MLSTM), tok(LANES),
                  pl.BlockSpec((1, N_GATES, chunk), lambda bi, i: (bi, 0, i)),
                  pl.BlockSpec((CONV_W, 2 * D_MLSTM), lambda bi, i: (0, 0))],
        out_specs=[tok(D_MLSTM),
                   pl.BlockSpec((1, MLSTM_HEADS, MLSTM_HEAD_DIM, MLSTM_HEAD_DIM), lambda bi, i: (bi, 0, 0, 0)),
                   pl.BlockSpec((1, MLSTM_HEADS, MLSTM_HEAD_DIM), lambda bi, i: (bi, 0, 0)),
                   pl.BlockSpec((1, 1, LANES), lambda bi, i: (bi, 0, 0))],
        out_shape=[jax.ShapeDtypeStruct((b, s, D_MLSTM), F32),
                   jax.ShapeDtypeStruct((b, MLSTM_HEADS, MLSTM_HEAD_DIM, MLSTM_HEAD_DIM), F32),
                   jax.ShapeDtypeStruct((b, MLSTM_HEADS, MLSTM_HEAD_DIM), F32),
                   jax.ShapeDtypeStruct((b, 1, LANES), F32)],
        scratch_shapes=[pltpu.VMEM((chunk, 2 * D_MLSTM), F32)],
        compiler_params=_cparams(("arbitrary", "arbitrary")),
        name="mlstm_prompt",
    )(qk_m, v_m, if_m, if_t, conv_w)


def _back_body(x_ref, a_ref, hm_ref, om_ref, g1_ref, sc_ref, sh_ref, ag_ref, mg_ref, wo_ref, n2_ref, wq_ref,
               x1_ref, h2_ref, pq_ref):
    a = a_ref[0]
    tm = a.shape[0]
    lane = _iota((tm, LANES), 1)
    low = lane < ATTN_HEAD_DIM
    proj = jnp.zeros((tm, D_MODEL), F32)
    for t in range(D_ATTN // LANES):
        sl = slice(t * LANES, (t + 1) * LANES)
        at = a[:, sl]
        sq = at * at
        s_lo = jnp.sum(jnp.where(low, sq, 0.0), axis=-1, keepdims=True)
        s_hi = jnp.sum(jnp.where(low, 0.0, sq), axis=-1, keepdims=True)
        scale = jnp.where(low, lax.rsqrt(s_lo / ATTN_HEAD_DIM + EPS), lax.rsqrt(s_hi / ATTN_HEAD_DIM + EPS))
        an = at * scale * ag_ref[:, sl]
        proj = proj + jnp.dot(an.astype(BF16), wo_ref[sl, :], preferred_element_type=F32)
    hm = hm_ref[0]
    om = om_ref[0]
    for t in range(MLSTM_HEADS):
        sl = slice(t * LANES, (t + 1) * LANES)
        ht = hm[:, sl]
        hn = ht * lax.rsqrt(jnp.mean(ht * ht, axis=-1, keepdims=True) + EPS) * mg_ref[:, sl]
        mm = jax.nn.sigmoid(om[:, sl]) * hn
        proj = proj + jnp.dot(mm.astype(BF16), wo_ref[D_ATTN + t * LANES:D_ATTN + (t + 1) * LANES, :],
                              preferred_element_type=F32)
    x1 = x_ref[0] + g1_ref[0] * proj
    x1_ref[0] = x1
    h2 = x1 * lax.rsqrt(jnp.mean(x1 * x1, axis=-1, keepdims=True) + EPS) * n2_ref[...]
    h2 = h2 * (1.0 + sc_ref[0]) + sh_ref[0]
    h2_ref[0] = h2
    pq_ref[0] = jnp.dot(h2.astype(BF16), wq_ref[...], preferred_element_type=F32).astype(BF16)


def _back(x, attn, h_m, o_m, g1, sc2, sh2, attn_g, mlstm_g, w_out, norm2_g, w_query, tm):
    b, s, d = x.shape
    per_row = g1.shape[1] != 1
    mod_spec = (pl.BlockSpec((1, tm, d), lambda bi, i: (bi, i, 0)) if per_row
                else pl.BlockSpec((1, 1, d), lambda bi, i: (bi, 0, 0)))
    nq = w_query.shape[1]
    const = lambda shape: pl.BlockSpec(shape, lambda bi, i: (0,) * len(shape))
    tok = lambda w: pl.BlockSpec((1, tm, w), lambda bi, i: (bi, i, 0))
    return pl.pallas_call(
        _back_body,
        grid=(b, s // tm),
        in_specs=[tok(d), tok(D_ATTN), tok(D_MLSTM), tok(D_MLSTM), mod_spec, mod_spec, mod_spec,
                  const((1, D_ATTN)), const((1, D_MLSTM)), const((d, d)), const((1, d)), const((d, nq))],
        out_specs=[tok(d), tok(d), tok(nq)],
        out_shape=[jax.ShapeDtypeStruct((b, s, d), F32), jax.ShapeDtypeStruct((b, s, d), F32),
                   jax.ShapeDtypeStruct((b, s, nq), BF16)],
        compiler_params=_cparams(("arbitrary", "arbitrary")),
        name="back",
    )(x, attn, h_m, o_m, g1, sc2, sh2, attn_g.reshape(1, D_ATTN), mlstm_g.reshape(1, D_MLSTM),
      w_out.astype(BF16), norm2_g.reshape(1, d), w_query.astype(BF16))


def _peer_route_body(pq_ref, sk_ref, idx_ref, gate_ref):
    tp = pq_ref.shape[0]
    k = PEER_TOPK
    n_exp = PEER_N_KEYS * PEER_N_KEYS
    out_row = _iota((k, tp), 0)

    def extract(score, tag, fill):
        vals = jnp.zeros((k, tp), F32)
        tags = jnp.zeros((k, tp), F32)
        for r in range(k):
            m = jnp.max(score, axis=0, keepdims=True)
            t = jnp.min(jnp.where(score == m, tag, fill), axis=0, keepdims=True)
            vals = jnp.where(out_row == r, m, vals)
            tags = jnp.where(out_row == r, t, tags)
            score = jnp.where(tag == t, -jnp.inf, score)
        return vals, tags

    key_row = _iota((PEER_N_KEYS, tp), 0).astype(F32)
    pos = _iota((k * k, tp), 0).astype(F32)
    gates, ids = [], []
    for h in range(PEER_HEADS):
        halves = []
        for p in range(2):
            c0 = (2 * h + p) * PEER_HALF
            qh = pq_ref[:, c0:c0 + PEER_HALF].astype(BF16)
            s = _nt(sk_ref[2 * h + p].astype(BF16), qh)
            halves.append(extract(s, key_row, float(PEER_N_KEYS)))
        (v0, i0), (v1, i1) = halves
        cand = jnp.concatenate([v0[a:a + 1] + v1 for a in range(k)], axis=0)
        eid = jnp.concatenate([i0[a:a + 1] * PEER_N_KEYS + i1 for a in range(k)], axis=0)
        comb = pos * n_exp + eid
        c_s, c_key = extract(cand, comb, float(k * k * n_exp))
        e = jnp.exp(c_s - c_s[0:1])
        gates.append(e / jnp.sum(e, axis=0, keepdims=True))
        ids.append(c_key)
    gate_ref[...] = jnp.concatenate(gates, axis=0).T
    keys = jnp.concatenate(ids, axis=0).T.astype(I32)
    idx_ref[...] = keys & (n_exp - 1)


def _peer_route(pq, sub_keys):
    n = pq.shape[0]
    tp = LANES
    n_pad = -(-n // tp) * tp
    pq = jnp.pad(pq, ((0, n_pad - n), (0, 0)))
    sk = sub_keys.reshape(PEER_HEADS * 2, PEER_N_KEYS, PEER_HALF)
    idx, gate = pl.pallas_call(
        _peer_route_body,
        grid=(n_pad // tp,),
        in_specs=[pl.BlockSpec((tp, pq.shape[1]), lambda i: (i, 0)),
                  pl.BlockSpec(sk.shape, lambda i: (0, 0, 0))],
        out_specs=[pl.BlockSpec((tp, LANES), lambda i: (i, 0)), pl.BlockSpec((tp, LANES), lambda i: (i, 0))],
        out_shape=[jax.ShapeDtypeStruct((n_pad, LANES), I32), jax.ShapeDtypeStruct((n_pad, LANES), F32)],
        compiler_params=_cparams(("arbitrary",)),
        name="peer_route",
    )(pq, sk)
    return idx[:n], gate[:n]


N_SEL = PEER_HEADS * PEER_TOPK
EXPERT_SLOTS = 8


def _pack_bf16_pairs(a):
    half = a.shape[1] // 2
    b = lax.bitcast_convert_type(a.astype(BF16), jnp.uint16).astype(jnp.uint32)
    return (b[:, :half] << 16) | b[:, half:]


def _peer_expert_body(idx_ref, gate_ref, h2_ref, x1_ref, g2_ref, nf_ref, uv_ref, y_ref, *scratch):
    n_slots = EXPERT_SLOTS
    bufs = scratch[:n_slots]
    sem, peer_scr = scratch[n_slots], scratch[n_slots + 1]
    tt = h2_ref.shape[1]
    half = D_MODEL // 2

    def issue(t, slot):
        for k in range(N_SEL):
            e = idx_ref[0, 0, t * N_SEL + k]
            pltpu.make_async_copy(uv_ref.at[e], bufs[slot].at[pl.ds(k, 1)],
                                  sem.at[slot]).start(priority=k % 2)

    def wait_all(slot):
        pltpu.make_async_copy(uv_ref.at[pl.ds(0, N_SEL), 0], bufs[slot], sem.at[slot]).wait()

    rr = _iota((N_SEL, LANES), 0)
    cc = _iota((N_SEL, LANES), 1)
    diag = rr == cc

    def unpack(words):
        hi = lax.bitcast_convert_type(words & jnp.uint32(0xFFFF0000), F32)
        lo = lax.bitcast_convert_type(words << 16, F32)
        return hi, lo

    def compute(t, slot):
        x = h2_ref[0, pl.ds(t, 1), :]
        u_hi, u_lo = unpack(bufs[slot][:, 0:half])
        s = jnp.sum(u_hi * x[:, 0:half] + u_lo * x[:, half:D_MODEL], axis=-1, keepdims=True)
        act = 0.5 * s * (1.0 + lax.erf(s * (0.5 ** 0.5)))
        g_row = gate_ref[0, pl.ds(t, 1), :]
        g_col = jnp.sum(jnp.where(diag, jnp.broadcast_to(g_row, (N_SEL, LANES)), 0.0), axis=-1, keepdims=True)
        w = g_col * act
        v_hi, v_lo = unpack(bufs[slot][:, half:2 * half])
        peer_scr[pl.ds(t, 1), 0:half] = jnp.sum(v_hi * w, axis=0, keepdims=True)
        peer_scr[pl.ds(t, 1), half:D_MODEL] = jnp.sum(v_lo * w, axis=0, keepdims=True)

    ahead = n_slots - 2
    for t in range(ahead):
        issue(t, t)

    def token_group(t0, last):
        for sl in range(n_slots):
            if not last or sl + ahead < n_slots:
                issue(t0 + sl + ahead, (sl + ahead) % n_slots)
            wait_all(sl)
            compute(t0 + sl, sl)

    def steady(i, carry):
        token_group(n_slots * i, False)
        return carry

    n_groups = tt // n_slots
    lax.fori_loop(0, n_groups - 1, steady, 0)
    token_group(n_slots * (n_groups - 1), True)
    y = x1_ref[0] + g2_ref[0] * peer_scr[...]
    y_ref[0] = y * lax.rsqrt(jnp.mean(y * y, axis=-1, keepdims=True) + EPS) * nf_ref[...]


def _peer_experts(idx, gate, h2, x1, g2, normf_g, uv, tt):
    b, s, d = h2.shape
    n_slots = EXPERT_SLOTS
    assert tt % n_slots == 0 and tt >= n_slots and s % tt == 0
    nt = s // tt
    per_row = g2.shape[1] != 1
    mod_spec = (pl.BlockSpec((1, tt, d), lambda bi, i: (bi, i, 0)) if per_row
                else pl.BlockSpec((1, 1, d), lambda bi, i: (bi, 0, 0)))
    tok = lambda w: pl.BlockSpec((1, tt, w), lambda bi, i: (bi, i, 0))
    idx3 = idx.reshape(b * nt, 1, tt * N_SEL)
    return pl.pallas_call(
        _peer_expert_body,
        grid=(b, nt),
        in_specs=[pl.BlockSpec((1, 1, tt * N_SEL), lambda bi, i: (bi * nt + i, 0, 0), memory_space=pltpu.SMEM),
                  tok(LANES), tok(d), tok(d), mod_spec,
                  pl.BlockSpec((1, d), lambda bi, i: (0, 0)),
                  pl.BlockSpec(memory_space=pl.ANY)],
        out_specs=tok(d),
        out_shape=jax.ShapeDtypeStruct((b, s, d), F32),
        scratch_shapes=[pltpu.VMEM((N_SEL, uv.shape[2]), jnp.uint32)] * n_slots + [
                        pltpu.SemaphoreType.DMA((n_slots,)),
                        pltpu.VMEM((tt, d), F32)],
        compiler_params=_cparams(("arbitrary", "arbitrary")),
        name="peer_experts",
    )(idx3, gate.reshape(b, s, LANES), h2, x1, g2, normf_g.reshape(1, d), uv)


def _layer_tail(x, mods, attn, h_m, o_m, attn_g, mlstm_g, w_out, norm2_g, w_query, sub_keys, uv, normf_g, tm, tt):
    g1, sh2, sc2, g2 = mods
    b, s, d = x.shape
    x1, h2, pq = _back(x, attn, h_m, o_m, g1, sc2, sh2, attn_g, mlstm_g, w_out, norm2_g, w_query, tm)
    idx, gate = _peer_route(pq.reshape(b * s, -1), sub_keys)
    return _peer_experts(idx, gate, h2, x1, g2, normf_g, uv, tt)


def _prompt_path(x, mod6, p):
    b, s, d = x.shape
    sh1, sc1, g1, sh2, sc2, g2 = [m.reshape(b, 1, d) for m in mod6]
    tm = min(512, s)
    q, k, v, qb, kb, vb, qk_m, v_m, o_m, if_m, if_t = _front(x, sc1, sh1, p["norm1_g"], p["w_in"], p["b_if"], tm)
    attn = _moba_prompt(q, k, qb, kb, vb)
    h_m, c_p, n_p, m_p = _mlstm_prompt(qk_m, v_m, if_m, if_t, p["conv_w"], min(256, s))
    y = _layer_tail(x, (g1, sh2, sc2, g2), attn, h_m, o_m, p["attn_norm_g"], p["mlstm_norm_g"], p["w_out"],
                    p["norm2_g"], p["peer_w_query"], p["peer_sub_keys"], p["uv"], p["normf_g"],
                    min(256, s), min(64, s))
    conv_p = qk_m[:, s - (CONV_W - 1):, :]
    return y, k, v, c_p, n_p, m_p[:, 0, :MLSTM_HEADS], conv_p


PAGES_PER_STEP = 16


def _page_sum_body(pt_ref, *refs, n_in, ppb):
    del pt_ref
    g = pl.program_id(1)
    out_ref = refs[n_in]

    @pl.when(g == 0)
    def _():
        out_ref[...] = jnp.zeros_like(out_ref)

    lane = _iota((D_ATTN, LANES), 1)
    acc = out_ref[0]
    for j in range(0, n_in, ppb):
        pages = refs[j][0, 0]
        for jj in range(1, ppb):
            pages = pages + refs[j + jj][0, 0]
        block_sum = jnp.sum(pages.reshape(D_ATTN, pages.shape[-1]), axis=-1, keepdims=True)
        acc = jnp.where(lane == (g * n_in + j) // ppb, block_sum, acc)
    out_ref[0] = acc


def _page_sums(kp5, pt_flat, layer, db, n_pages, ppb):
    assert n_pages <= LANES
    n_in = min(PAGES_PER_STEP, n_pages)
    assert n_pages % n_in == 0 and n_in % ppb == 0
    ps = kp5.shape[-1]

    def page_spec(j):
        return pl.BlockSpec((1, 1, ATTN_HEADS, ATTN_HEAD_DIM, ps),
                            lambda d, g, pt: (layer, pt[d * n_pages + g * n_in + j], 0, 0, 0))

    return pl.pallas_call(
        functools.partial(_page_sum_body, n_in=n_in, ppb=ppb),
        grid_spec=pltpu.PrefetchScalarGridSpec(
            num_scalar_prefetch=1,
            grid=(db, n_pages // n_in),
            in_specs=[page_spec(j) for j in range(n_in)],
            out_specs=pl.BlockSpec((1, D_ATTN, LANES), lambda d, g, pt: (d, 0, 0)),
        ),
        out_shape=jax.ShapeDtypeStruct((db, D_ATTN, LANES), F32),
        compiler_params=_cparams(("arbitrary", "arbitrary")),
        name="sample_page_sums",
    )(pt_flat, *([kp5] * n_in))


def _sample_select_body(q_ref, ks_ref, o_ref, *, nbp):
    q = jnp.broadcast_to(q_ref[0], (ATTN_HEADS, D_ATTN))
    qbd = jnp.where(_iota(q.shape, 1) // ATTN_HEAD_DIM == _iota(q.shape, 0), q, 0.0)
    s = jnp.dot(qbd, ks_ref[0], precision=HI, preferred_element_type=F32) * (1.0 / MOBA_BLOCK)
    lane = _iota(s.shape, 1)
    s = jnp.where(lane < nbp, s, -jnp.inf)
    top = jnp.zeros(s.shape, I32)
    for r in range(min(MOBA_TOPK, nbp)):
        m = jnp.max(s, axis=-1, keepdims=True)
        i = jnp.min(jnp.where(s == m, lane, LANES), axis=-1, keepdims=True)
        top = jnp.where(lane == r, i, top)
        s = jnp.where(lane == i, -jnp.inf, s)
    o_ref[0] = top


def _sample_select(q, ksum, nbp):
    db = q.shape[0]
    return pl.pallas_call(
        functools.partial(_sample_select_body, nbp=nbp),
        grid=(db,),
        in_specs=[pl.BlockSpec((1, 1, D_ATTN), lambda d: (d, 0, 0)),
                  pl.BlockSpec((1, D_ATTN, LANES), lambda d: (d, 0, 0))],
        out_specs=pl.BlockSpec((1, ATTN_HEADS, LANES), lambda d: (d, 0, 0)),
        out_shape=jax.ShapeDtypeStruct((db, ATTN_HEADS, LANES), I32),
        compiler_params=_cparams(("arbitrary",)),
        name="sample_select",
    )(q.reshape(db, 1, D_ATTN), ksum)


def _row_to_col(row):
    eye = _iota((LANES, LANES), 0) == _iota((LANES, LANES), 1)
    return jnp.sum(jnp.where(eye, jnp.broadcast_to(row, (LANES, LANES)), 0.0), axis=-1, keepdims=True)


def _col_to_row(col):
    eye = _iota((LANES, LANES), 0) == _iota((LANES, LANES), 1)
    return jnp.sum(jnp.where(eye, jnp.broadcast_to(col, (LANES, LANES)), 0.0), axis=0, keepdims=True)


def _sample_attend_body(pt_ref, top_ref, q_ref, kn_ref, vn_ref, kp_ref, vp_ref, o_ref, kbuf, vbuf, sem,
                        *, layer, n_pages, n_sel, ppb):
    d = pl.program_id(0)
    per_head = n_sel * ppb

    def copies(h, r, j):
        blk = top_ref[(d * ATTN_HEADS + h) * n_sel + r]
        phys = pt_ref[d * n_pages + blk * ppb + j]
        slot = h * per_head + r * ppb + j
        return (pltpu.make_async_copy(kp_ref.at[layer, phys, h], kbuf.at[slot], sem.at[0]),
                pltpu.make_async_copy(vp_ref.at[layer, phys, h], vbuf.at[slot], sem.at[1]))

    todo = [(h, r, j) for h in range(ATTN_HEADS) for r in range(n_sel) for j in range(ppb)]
    for hrj in todo:
        for c in copies(*hrj):
            c.start()
    q_row = q_ref[0] * (ATTN_HEAD_DIM ** -0.5)
    k_row = kn_ref[0]
    v_row = vn_ref[0]
    tiles = D_ATTN // LANES
    q_col = jnp.concatenate([_row_to_col(q_row[:, t * LANES:(t + 1) * LANES]) for t in range(tiles)], axis=0)
    head_of_lane = _iota((1, D_ATTN), 1) // ATTN_HEAD_DIM
    qk_own = q_row * k_row
    for hrj in todo:
        for c in copies(*hrj):
            c.wait()
    acc_cols = []
    p_own_row = jnp.zeros((1, D_ATTN), F32)
    l_row = jnp.ones((1, D_ATTN), F32)
    for h in range(ATTN_HEADS):
        qc = q_col[h * ATTN_HEAD_DIM:(h + 1) * ATTN_HEAD_DIM]
        base = h * per_head
        s_pages = [jnp.sum(kbuf[base + g] * qc, axis=0, keepdims=True) for g in range(per_head)]
        s_own = jnp.sum(jnp.where(head_of_lane == h, qk_own, 0.0), axis=-1, keepdims=True)
        m = s_own
        for s in s_pages:
            m = jnp.maximum(m, jnp.max(s, axis=-1, keepdims=True))
        p_own = jnp.exp(s_own - m)
        l = p_own
        acc = jnp.zeros(kbuf.shape[1:], F32)
        for g, s in enumerate(s_pages):
            p = jnp.exp(s - m)
            l = l + jnp.sum(p, axis=-1, keepdims=True)
            acc = acc + vbuf[base + g] * p
        acc_cols.append(jnp.sum(acc, axis=-1, keepdims=True))
        p_own_row = jnp.where(head_of_lane == h, p_own, p_own_row)
        l_row = jnp.where(head_of_lane == h, l, l_row)
    acc_col = jnp.concatenate(acc_cols, axis=0)
    acc_row = jnp.concatenate([_col_to_row(acc_col[t * LANES:(t + 1) * LANES]) for t in range(tiles)], axis=1)
    o_ref[0] = (acc_row + p_own_row * v_row) / l_row


def _sample_attend(q, k_new, v_new, kp5, vp5, pt_flat, top_flat, layer, n_pages, n_sel, ppb):
    db = q.shape[0]
    ps = kp5.shape[-1]
    row = pl.BlockSpec((1, 1, D_ATTN), lambda d, pt, top: (d, 0, 0))
    n_buf = ATTN_HEADS * n_sel * ppb
    out = pl.pallas_call(
        functools.partial(_sample_attend_body, layer=layer, n_pages=n_pages, n_sel=n_sel, ppb=ppb),
        grid_spec=pltpu.PrefetchScalarGridSpec(
            num_scalar_prefetch=2,
            grid=(db,),
            in_specs=[row, row, row, pl.BlockSpec(memory_space=pl.ANY), pl.BlockSpec(memory_space=pl.ANY)],
            out_specs=row,
            scratch_shapes=[pltpu.VMEM((n_buf, ATTN_HEAD_DIM, ps), F32), pltpu.VMEM((n_buf, ATTN_HEAD_DIM, ps), F32),
                            pltpu.SemaphoreType.DMA((2,))],
        ),
        out_shape=jax.ShapeDtypeStruct((db, 1, D_ATTN), F32),
        compiler_params=_cparams(("arbitrary",)),
        name="sample_attend",
    )(pt_flat, top_flat, q.reshape(db, 1, D_ATTN), k_new.reshape(db, 1, D_ATTN), v_new.reshape(db, 1, D_ATTN),
      kp5, vp5)
    return out.reshape(db, D_ATTN)


def _moba_sample(q, k_new, v_new, k_pool, v_pool, page_table, layer):
    db = q.shape[0]
    ps = k_pool.shape[2]
    n_pages = page_table.shape[1]
    ppb = MOBA_BLOCK // ps
    nbp = (n_pages * ps) // MOBA_BLOCK
    assert nbp >= 1 and (n_pages - nbp * ppb) * ps == 0, "the cached rows must fill whole MoBA blocks"
    n_sel = min(MOBA_TOPK, nbp)
    kp5 = jnp.transpose(k_pool, (0, 1, 3, 4, 2))
    vp5 = jnp.transpose(v_pool, (0, 1, 3, 4, 2))
    pt_flat = page_table.reshape(-1)
    ksum = _page_sums(kp5, pt_flat, layer, db, n_pages, ppb)
    top = _sample_select(q, ksum, nbp)
    top_flat = top[:, :, :n_sel].reshape(-1)
    return _sample_attend(q, k_new, v_new, kp5, vp5, pt_flat, top_flat, layer, n_pages, n_sel, ppb)


def _mlstm_step_body(qk_ref, vm_ref, if_ref, conv_ref, cw_ref, c_ref, n_ref, m_ref,
                     h_ref, c_out, n_out, m_out):
    tb = qk_ref.shape[0]
    w = cw_ref[...]
    lane = _iota((1, LANES), 1)

    def one(r):
        x = qk_ref[pl.ds(r, 1), :]
        cb = conv_ref[pl.ds(r, 1)][0]
        y = cb[0:1] * w[0:1]
        for j in range(1, CONV_W - 1):
            y = y + cb[j:j + 1] * w[j:j + 1]
        y = y + x * w[CONV_W - 1:CONV_W]
        qkc = _silu(y)
        gates = if_ref[pl.ds(r, 1), :]
        m_all = m_ref[pl.ds(r, 1), :]
        v_all = vm_ref[pl.ds(r, 1), :]
        for h in range(MLSTM_HEADS):
            sl = slice(h * MLSTM_HEAD_DIM, (h + 1) * MLSTM_HEAD_DIM)
            q = qkc[:, sl]
            k = qkc[:, D_MLSTM + h * MLSTM_HEAD_DIM:D_MLSTM + (h + 1) * MLSTM_HEAD_DIM] * (MLSTM_HEAD_DIM ** -0.5)
            v = v_all[:, sl]
            ig = gates[:, h:h + 1]
            fg = gates[:, MLSTM_HEADS + h:MLSTM_HEADS + h + 1]
            m_prev = m_all[:, h:h + 1]
            inter = _log_sigmoid(fg) + m_prev
            m_t = jnp.maximum(inter, ig)
            wgt = jnp.exp(ig - m_t)
            w0 = jnp.exp(inter - m_t)
            c_prev = c_ref[pl.ds(r, 1), h][0]
            n_prev = n_ref[pl.ds(r, 1), h]
            sqk = jnp.sum(q * k, axis=-1, keepdims=True) * wgt
            cq = _nt(jnp.broadcast_to(q, (SUBLANES, MLSTM_HEAD_DIM)), c_prev, precision=HI)[0:1]
            num = w0 * cq + sqk * v
            den = w0 * jnp.sum(n_prev * q, axis=-1, keepdims=True) + sqk
            h_ref[pl.ds(r, 1), sl] = num / jnp.maximum(jnp.abs(den), jnp.exp(-m_t))
            c_out[pl.ds(r, 1), h] = (w0 * c_prev + (wgt * _row_to_col(v)) * k)[None]
            n_out[pl.ds(r, 1), h] = w0 * n_prev + wgt * k
            m_all = jnp.where(lane == h, m_t, m_all)
        m_out[pl.ds(r, 1), :] = m_all

    for r in range(tb):
        one(r)


def _mlstm_step(qk_m, v_m, if_m, conv_state, conv_w, c, n, m_pad, tb):
    db = qk_m.shape[0]
    row = lambda w: pl.BlockSpec((tb, w), lambda i: (i, 0))
    c_spec = pl.BlockSpec((tb, MLSTM_HEADS, MLSTM_HEAD_DIM, MLSTM_HEAD_DIM), lambda i: (i, 0, 0, 0))
    n_spec = pl.BlockSpec((tb, MLSTM_HEADS, MLSTM_HEAD_DIM), lambda i: (i, 0, 0))
    return pl.pallas_call(
        _mlstm_step_body,
        grid=(db // tb,),
        in_specs=[row(2 * D_MLSTM), row(D_MLSTM), row(LANES),
                  pl.BlockSpec((tb, CONV_W - 1, 2 * D_MLSTM), lambda i: (i, 0, 0)),
                  pl.BlockSpec((CONV_W, 2 * D_MLSTM), lambda i: (0, 0)), c_spec, n_spec, row(LANES)],
        out_specs=[row(D_MLSTM), c_spec, n_spec, row(LANES)],
        out_shape=[jax.ShapeDtypeStruct((db, D_MLSTM), F32), jax.ShapeDtypeStruct(c.shape, F32),
                   jax.ShapeDtypeStruct(n.shape, F32), jax.ShapeDtypeStruct((db, LANES), F32)],
        compiler_params=_cparams(("arbitrary",)),
        name="mlstm_step",
    )(qk_m, v_m, if_m, conv_state, conv_w, c, n, m_pad)


def _sample_path(x, mod6, p, k_pool, v_pool, page_table, c0, n0, m0, conv0, layer):
    db, t, d = x.shape
    assert t == 1, "one new token per sequence"
    assert db % SUBLANES == 0
    xs = x.reshape(1, db, d)
    sh1, sc1, g1, sh2, sc2, g2 = [m.reshape(1, db, d) for m in mod6]
    q, k, v, _, _, _, qk_m, v_m, o_m, if_m, _ = _front(xs, sc1, sh1, p["norm1_g"], p["w_in"], p["b_if"], db)
    q, k, v, qk_m, v_m, if_m = [a[0] for a in (q, k, v, qk_m, v_m, if_m)]
    attn = _moba_sample(q, k, v, k_pool, v_pool, page_table, layer)
    m_pad = jnp.pad(m0, ((0, 0), (0, LANES - MLSTM_HEADS)))
    h_s, c_s, n_s, m_s = _mlstm_step(qk_m, v_m, if_m, conv0, p["conv_w"], c0, n0, m_pad, SUBLANES)
    y = _layer_tail(xs, (g1, sh2, sc2, g2), attn[None], h_s[None], o_m, p["attn_norm_g"], p["mlstm_norm_g"],
                    p["w_out"], p["norm2_g"], p["peer_w_query"], p["peer_sub_keys"], p["uv"], p["normf_g"],
                    db, min(64, db))
    conv_s = jnp.concatenate([conv0[:, 1:], qk_m[:, None, :]], axis=1)
    return y.reshape(db, t, d), k, v, c_s, n_s, m_s[:, :MLSTM_HEADS], conv_s


def kernel(x_prompt, x_sample, cache_k_pool, cache_v_pool, page_table, state_mlstm_C, state_mlstm_n, state_mlstm_m, state_conv, c_prompt, c_sample, w_ada, b_ada, norm1_g, w_in, b_if, conv_w, attn_norm_g, mlstm_norm_g, w_out, norm2_g, peer_w_query, peer_sub_keys, peer_u, peer_v, normf_g):
    b, s, d = x_prompt.shape
    db = x_sample.shape[0]
    l = 0
    p = dict(norm1_g=norm1_g[l], w_in=w_in[l], b_if=b_if[l], conv_w=conv_w[l], attn_norm_g=attn_norm_g[l],
             mlstm_norm_g=mlstm_norm_g[l], w_out=w_out[l], norm2_g=norm2_g[l], peer_w_query=peer_w_query[l],
             peer_sub_keys=peer_sub_keys[l], normf_g=normf_g,
             uv=jnp.concatenate([_pack_bf16_pairs(peer_u[l]), _pack_bf16_pairs(peer_v[l])], axis=1)[:, None, :])
    rows = b + db
    pad = -rows % SUBLANES
    c_all = jnp.pad(jnp.concatenate([c_prompt, c_sample], axis=0), ((0, pad), (0, 0)))
    mod = _ada(c_all, w_ada[l], b_ada[l])
    mod_p = [mod[:b, i * d:(i + 1) * d] for i in range(6)]
    mod_s = [mod[b:rows, i * d:(i + 1) * d] for i in range(6)]
    y_p, k_p, v_p, c_p, n_p, m_p, conv_p = _prompt_path(x_prompt, mod_p, p)
    y_s, k_s, v_s, c_s, n_s, m_s, conv_s = _sample_path(
        x_sample, mod_s, p, cache_k_pool, cache_v_pool, page_table,
        state_mlstm_C[l], state_mlstm_n[l], state_mlstm_m[l], state_conv[l], l)
    hp = (b, s, ATTN_HEADS, ATTN_HEAD_DIM)
    hs = (db, 1, ATTN_HEADS, ATTN_HEAD_DIM)
    return (y_p, y_s, k_p.reshape(hp)[None], v_p.reshape(hp)[None], k_s.reshape(hs)[None], v_s.reshape(hs)[None],
            c_p[None], n_p[None], m_p[None], conv_p[None], c_s[None], n_s[None], m_s[None], conv_s[None])
```
